```python
import math
import jax, jax.numpy as jnp
from jax import lax
import numpy as np

D_MODEL = 1024
BATCH = 4
SEQ = 8192
DEPTH = 2

N_EVEN = (DEPTH + 1) // 2
N_ODD = DEPTH // 2

CONV_A_CH = D_MODEL // 2
CONV_A_WIDTH = 31
LN_EPS = 1e-5

HEAD_DIM = 64
HEADS_PER_GROUP = 8
DILATED_GROUPS = ((128, 1), (512, 4), (2048, 16))
N_GROUPS = len(DILATED_GROUPS)
N_ATTN_HEADS = N_GROUPS * HEADS_PER_GROUP
ATTN_OUT = HEADS_PER_GROUP * HEAD_DIM
ATTN_IN = 3 * N_ATTN_HEADS * HEAD_DIM
AB_IN = 2 * CONV_A_CH + ATTN_IN
AB_CAT = CONV_A_CH + ATTN_OUT

NUM_BUCKETS = 32
REL_MAX_DISTANCE = 2048

SC_WIDTH = D_MODEL
SC_CONV_WIDTH = 3

D_FF = 4 * D_MODEL
RMS_EPS = 1e-6
NEG_INF = -1e30

kernel_name = 'hybrid_conformer_dilated_shortconv'


def rms_norm(x, g, eps=RMS_EPS):
    xf = x.astype(jnp.float32)
    y = xf * lax.rsqrt(jnp.mean(xf * xf, axis=-1, keepdims=True) + eps)
    return (y * g.astype(jnp.float32)).astype(x.dtype)


def layer_norm(x, g, b, eps=LN_EPS):
    xf = x.astype(jnp.float32)
    mu = jnp.mean(xf, axis=-1, keepdims=True)
    var = jnp.mean(jnp.square(xf - mu), axis=-1, keepdims=True)
    y = (xf - mu) * lax.rsqrt(var + eps) * g.astype(jnp.float32) + b.astype(jnp.float32)
    return y.astype(x.dtype)


def causal_depthwise_conv(x, w):
    k_width, ch = w.shape
    xp = jnp.pad(x, ((0, 0), (k_width - 1, 0), (0, 0)))
    return lax.conv_general_dilated(
        xp, w[:, None, :].astype(x.dtype), window_strides=(1,), padding='VALID',
        dimension_numbers=('NWC', 'WIO', 'NWC'), feature_group_count=ch)


def t5_causal_bucket(n):
    max_exact = NUM_BUCKETS // 2
    nf = jnp.maximum(n, 1).astype(jnp.float32)
    large = max_exact + (jnp.log(nf / max_exact) / math.log(REL_MAX_DISTANCE / max_exact)
                         * (NUM_BUCKETS - max_exact)).astype(jnp.int32)
    return jnp.where(n < max_exact, n, jnp.minimum(large, NUM_BUCKETS - 1))


def dilated_group_attention(q, k, v, bias_table, window, dilation):
    bsz, seq, heads, dh = q.shape
    steps = window // dilation
    span = steps * dilation
    seq_pad = -(-seq // span) * span
    n_blk = seq_pad // span

    def to_blocks(t):
        t = jnp.pad(t, ((0, 0), (0, seq_pad - seq), (0, 0), (0, 0)))
        t = t.reshape(bsz, n_blk, steps, dilation, heads, dh)
        return t.transpose(0, 3, 4, 1, 2, 5)

    def with_prev(t):
        prev = jnp.pad(t[:, :, :, :-1], ((0, 0), (0, 0), (0, 0), (1, 0), (0, 0), (0, 0)))
        return jnp.concatenate([prev, t], axis=4)

    qb = to_blocks(q)
    kc = with_prev(to_blocks(k))
    vc = with_prev(to_blocks(v))
    scores = jnp.einsum('bdhnqe,bdhnke->bdhnqk', qb, kc,
                        preferred_element_type=jnp.float32) * (dh ** -0.5)

    a_idx = jnp.arange(steps)[:, None]
    c_idx = jnp.arange(2 * steps)[None, :]
    m = a_idx + steps - c_idx
    band = (m >= 0) & (m <= steps)
    first = (jnp.arange(n_blk) == 0)[:, None, None] & (c_idx < steps)[None]
    valid = band[None] & ~first
    bucket = t5_causal_bucket(jnp.clip(m, 0, steps) * dilation)
    bias = bias_table[bucket].astype(jnp.float32).transpose(2, 0, 1)

    logits = jnp.where(valid, scores + bias[:, None], NEG_INF)
    mx = jnp.max(logits, axis=-1, keepdims=True)
    p = jnp.exp(logits - mx)
    den = jnp.sum(p, axis=-1)
    out = jnp.einsum('bdhnqk,bdhnke->bdhnqe', p, vc.astype(jnp.float32)) / den[..., None]
    lse = mx[..., 0] + jnp.log(den)
    out = out.transpose(0, 3, 4, 1, 2, 5).reshape(bsz, seq_pad, heads, dh)[:, :seq]
    lse = lse.transpose(0, 3, 4, 1, 2).reshape(bsz, seq_pad, heads)[:, :seq]
    return out, lse


def dilated_mixture_attention(qkv, rel_bias):
    bsz, seq, _ = qkv.shape
    qkv = qkv.reshape(bsz, seq, 3, N_GROUPS, HEADS_PER_GROUP, HEAD_DIM)
    outs, lses = [], []
    for g, (window, dilation) in enumerate(DILATED_GROUPS):
        o, l = dilated_group_attention(
            qkv[:, :, 0, g], qkv[:, :, 1, g], qkv[:, :, 2, g],
            rel_bias[:, g * HEADS_PER_GROUP:(g + 1) * HEADS_PER_GROUP], window, dilation)
        outs.append(o)
        lses.append(l)
    wts = jax.nn.softmax(jnp.stack(lses, axis=0), axis=0)
    out = jnp.sum(wts[..., None] * jnp.stack(outs, axis=0), axis=0)
    return out.reshape(bsz, seq, ATTN_OUT)


def conformer_conv_module(u, conv_w, conv_b, ln_g, ln_b):
    val, gate = jnp.split(u, 2, axis=-1)
    h = val * jax.nn.sigmoid(gate)
    h = causal_depthwise_conv(h, conv_w) + conv_b.astype(h.dtype)
    h = layer_norm(h, ln_g, ln_b)
    return jax.nn.silu(h)


def short_gated_conv(u, conv_w):
    b_gate, c_gate, val = jnp.split(u, 3, axis=-1)
    return b_gate * causal_depthwise_conv(c_gate * val, conv_w)


def setup_inputs(seed: int = 0) -> dict:
    key = jax.random.key(seed)
    ks = jax.random.split(key, 17)
    D = D_MODEL

    def nrm(k, shape, scale):
        return scale * jax.random.normal(k, shape, jnp.float32)

    return {
        'x': nrm(ks[0], (BATCH, SEQ, D), 1.0),
        'rel_bias': nrm(ks[1], (NUM_BUCKETS, N_ATTN_HEADS), 0.5),
        'ab_norm': 1.0 + nrm(ks[2], (N_EVEN, D), 0.01),
        'ab_w_in': nrm(ks[3], (N_EVEN, D, AB_IN), D ** -0.5),
        'ab_conv_w': nrm(ks[4], (N_EVEN, CONV_A_WIDTH, CONV_A_CH), CONV_A_WIDTH ** -0.5),
        'ab_conv_b': nrm(ks[5], (N_EVEN, CONV_A_CH), 0.01),
        'ab_ln_g': 1.0 + nrm(ks[6], (N_EVEN, CONV_A_CH), 0.01),
        'ab_ln_b': nrm(ks[7], (N_EVEN, CONV_A_CH), 0.01),
        'ab_w_out': nrm(ks[8], (N_EVEN, AB_CAT, D), AB_CAT ** -0.5),
        'sc_norm': 1.0 + nrm(ks[9], (N_ODD, D), 0.01),
        'sc_w_in': nrm(ks[10], (N_ODD, D, 3 * SC_WIDTH), D ** -0.5),
        'sc_conv_w': nrm(ks[11], (N_ODD, SC_CONV_WIDTH, SC_WIDTH), SC_CONV_WIDTH ** -0.5),
        'sc_w_out': nrm(ks[12], (N_ODD, SC_WIDTH, D), SC_WIDTH ** -0.5),
        'mlp_norm': 1.0 + nrm(ks[13], (DEPTH, D), 0.01),
        'mlp_w_up': nrm(ks[14], (DEPTH, D, D_FF), D ** -0.5),
        'mlp_w_down': nrm(ks[15], (DEPTH, D_FF, D), D_FF ** -0.5),
        'final_norm': 1.0 + nrm(ks[16], (D,), 0.01),
    }


def reference(x, rel_bias, ab_norm, ab_w_in, ab_conv_w, ab_conv_b, ab_ln_g, ab_ln_b,
              ab_w_out, sc_norm, sc_w_in, sc_conv_w, sc_w_out, mlp_norm, mlp_w_up,
              mlp_w_down, final_norm):
    h = x
    for layer in range(DEPTH):
        i = layer // 2
        if layer % 2 == 0:
            u = jnp.einsum('bsd,de->bse', rms_norm(h, ab_norm[i]), ab_w_in[i])
            ya = conformer_conv_module(u[..., :2 * CONV_A_CH], ab_conv_w[i], ab_conv_b[i],
                                       ab_ln_g[i], ab_ln_b[i])
            yb = dilated_mixture_attention(u[..., 2 * CONV_A_CH:], rel_bias).astype(ya.dtype)
            y = jnp.einsum('bsc,cd->bsd', jnp.concatenate([ya, yb], axis=-1), ab_w_out[i])
        else:
            u = jnp.einsum('bsd,de->bse', rms_norm(h, sc_norm[i]), sc_w_in[i])
            y = jnp.einsum('bsc,cd->bsd', short_gated_conv(u, sc_conv_w[i]), sc_w_out[i])
        h = h + y.astype(h.dtype)
        z = jnp.einsum('bsd,df->bsf', rms_norm(h, mlp_norm[layer]), mlp_w_up[layer])
        z = jnp.square(jax.nn.relu(z))
        h = h + jnp.einsum('bsf,fd->bsd', z, mlp_w_down[layer]).astype(h.dtype)
    return rms_norm(h, final_norm)
```

```python
import functools
import math

import jax
import jax.numpy as jnp
from jax import lax
from jax.experimental import pallas as pl
from jax.experimental.pallas import tpu as pltpu

D_MODEL = 1024
CONV_A_CH = 512
CONV_A_WIDTH = 31
LN_EPS = 1e-5
HEAD_DIM = 64
HEADS_PER_GROUP = 8
DILATED_GROUPS = ((128, 1), (512, 4), (2048, 16))
N_GROUPS = 3
ATTN_OUT = HEADS_PER_GROUP * HEAD_DIM
ATTN_IN = 3 * N_GROUPS * ATTN_OUT
NUM_BUCKETS = 32
REL_MAX_DISTANCE = 2048
SC_CONV_WIDTH = 3
D_FF = 4 * D_MODEL
RMS_EPS = 1e-6
NEG_INF = -1e30

STEPS = 128
LANES = 128
VMEM_LIMIT = 56 * 1024 * 1024

TM = 512
TJ = 512
CONV_HALO = 32
CONV_ROWS = 64
FF_CHUNK = 1024

BF16 = jnp.bfloat16
F32 = jnp.float32


def _dot(a, b):
    return jnp.dot(a, b, preferred_element_type=F32)


def _dot_nt(a, b):
    return lax.dot_general(a, b, (((1,), (1,)), ((), ())), preferred_element_type=F32)


def _rms_norm(x, g):
    y = x * lax.rsqrt(jnp.mean(x * x, axis=-1, keepdims=True) + RMS_EPS)
    return y * g


def _mlp(h, g_ref, wup_ref, wdown_ref, z_scr):
    xn = _rms_norm(h, g_ref[...]).astype(BF16)
    for c in range(D_FF // FF_CHUNK):
        cols = slice(c * FF_CHUNK, (c + 1) * FF_CHUNK)
        z = jnp.maximum(_dot(xn, wup_ref[:, cols]), 0.0)
        z_scr[:, cols] = (z * z).astype(BF16)
    return h + _dot(z_scr[...], wdown_ref[...])


def _inproj_ab_kernel(x_ref, g_ref, w_ref, glu_ref, qkv_ref):
    xn = _rms_norm(x_ref[0], g_ref[...]).astype(BF16)
    val = _dot(xn, w_ref[:, :CONV_A_CH])
    gate = _dot(xn, w_ref[:, CONV_A_CH:2 * CONV_A_CH])
    glu_ref[0] = val * jax.nn.sigmoid(gate)
    for c in range(ATTN_IN // ATTN_OUT):
        lo = 2 * CONV_A_CH + c * ATTN_OUT
        acc = _dot(xn, w_ref[:, lo:lo + ATTN_OUT])
        if c < N_GROUPS:
            acc = acc * (HEAD_DIM ** -0.5)
        qkv_ref[0, :, c * ATTN_OUT:(c + 1) * ATTN_OUT] = acc.astype(BF16)


def _inproj_ab(h, g, w):
    bsz, seq, d = h.shape
    return pl.pallas_call(
        _inproj_ab_kernel,
        grid=(bsz, seq // TM),
        in_specs=[
            pl.BlockSpec((1, TM, d), lambda b, s: (b, s, 0)),
            pl.BlockSpec((1, d), lambda b, s: (0, 0)),
            pl.BlockSpec(w.shape, lambda b, s: (0, 0)),
        ],
        out_specs=[
            pl.BlockSpec((1, TM, CONV_A_CH), lambda b, s: (b, s, 0)),
            pl.BlockSpec((1, TM, ATTN_IN), lambda b, s: (b, s, 0)),
        ],
        out_shape=[
            jax.ShapeDtypeStruct((bsz, seq, CONV_A_CH), F32),
            jax.ShapeDtypeStruct((bsz, seq, ATTN_IN), BF16),
        ],
        compiler_params=pltpu.CompilerParams(
            dimension_semantics=("arbitrary", "arbitrary"),
            vmem_limit_bytes=VMEM_LIMIT),
        name="inproj_ab",
    )(h, g, w)


def _attn_kernel(q_ref, k_ref, kp_ref, v_ref, vp_ref, bias_ref, o_ref, lse_ref):
    first_tile = pl.program_id(2) == 0
    lane = lax.broadcasted_iota(jnp.int32, (STEPS, LANES), 1)
    for i in range(TJ // STEPS):
        rows = slice(i * STEPS, (i + 1) * STEPS)
        q = q_ref[0, rows, :]
        kc = k_ref[0, rows, :]
        vc = v_ref[0, rows, :]
        if i == 0:
            kp = kp_ref[0]
            vp = vp_ref[0]
        else:
            prev = slice((i - 1) * STEPS, i * STEPS)
            kp = k_ref[0, prev, :]
            vp = v_ref[0, prev, :]
        lse_tile = jnp.zeros((STEPS, LANES), F32)
        outs = []
        for h in range(HEADS_PER_GROUP):
            cols = slice(h * HEAD_DIM, (h + 1) * HEAD_DIM)
            qh = q[:, cols]
            sp = _dot_nt(qh, kp[:, cols]) + bias_ref[h, :, :STEPS]
            if i == 0:
                sp = jnp.where(first_tile, NEG_INF, sp)
            sc = _dot_nt(qh, kc[:, cols]) + bias_ref[h, :, STEPS:]
            m = jnp.maximum(jnp.max(sp, axis=-1, keepdims=True),
                            jnp.max(sc, axis=-1, keepdims=True))
            pp = jnp.exp(sp - m)
            pc = jnp.exp(sc - m)
            den = jnp.sum(pp, axis=-1, keepdims=True) + jnp.sum(pc, axis=-1, keepdims=True)
            o = _dot(pp.astype(BF16), vp[:, cols]) + _dot(pc.astype(BF16), vc[:, cols])
            outs.append(o / den)
            lse_tile = jnp.where(lane == h, m + jnp.log(den), lse_tile)
        o_ref[0, rows, :] = jnp.concatenate(outs, axis=-1).astype(BF16)
        lse_ref[0, rows, :] = lse_tile


def _attention_group(qkv, bias, g, dilation):
    bsz, seq, _ = qkv.shape
    sd = seq // dilation
    view = qkv.reshape(bsz, sd, dilation * ATTN_IN)
    nblk = ATTN_IN // ATTN_OUT
    per_tile = TJ // STEPS

    def col(offset):
        return lambda b, r, t: (b, t, r * nblk + offset)

    def col_prev(offset):
        return lambda b, r, t: (b, jnp.maximum(t * per_tile - 1, 0), r * nblk + offset)

    o, lse = pl.pallas_call(
        _attn_kernel,
        grid=(bsz, dilation, sd // TJ),
        in_specs=[
            pl.BlockSpec((1, TJ, ATTN_OUT), col(g)),
            pl.BlockSpec((1, TJ, ATTN_OUT), col(N_GROUPS + g)),
            pl.BlockSpec((1, STEPS, ATTN_OUT), col_prev(N_GROUPS + g)),
            pl.BlockSpec((1, TJ, ATTN_OUT), col(2 * N_GROUPS + g)),
            pl.BlockSpec((1, STEPS, ATTN_OUT), col_prev(2 * N_GROUPS + g)),
            pl.BlockSpec(bias.shape, lambda b, r, t: (0, 0, 0)),
        ],
        out_specs=[
            pl.BlockSpec((1, TJ, ATTN_OUT), lambda b, r, t: (b, t, r)),
            pl.BlockSpec((1, TJ, LANES), lambda b, r, t: (b, t, r)),
        ],
        out_shape=[
            jax.ShapeDtypeStruct((bsz, sd, dilation * ATTN_OUT), BF16),
            jax.ShapeDtypeStruct((bsz, sd, dilation * LANES), F32),
        ],
        compiler_params=pltpu.CompilerParams(
            dimension_semantics=("arbitrary", "arbitrary", "arbitrary"),
            vmem_limit_bytes=VMEM_LIMIT),
        name=f"attn_d{dilation}",
    )(view, view, view, view, view, bias)
    return o.reshape(bsz, seq, ATTN_OUT), lse.reshape(bsz, seq, LANES)


def _t5_causal_bucket(n):
    max_exact = NUM_BUCKETS // 2
    nf = jnp.maximum(n, 1).astype(F32)
    large = max_exact + (jnp.log(nf / max_exact) / math.log(REL_MAX_DISTANCE / max_exact)
                         * (NUM_BUCKETS - max_exact)).astype(jnp.int32)
    return jnp.where(n < max_exact, n, jnp.minimum(large, NUM_BUCKETS - 1))


def _group_bias(rel_bias, g, dilation):
    a_idx = jnp.arange(STEPS)[:, None]
    c_idx = jnp.arange(2 * STEPS)[None, :]
    m = a_idx + STEPS - c_idx
    band = (m >= 0) & (m <= STEPS)
    bucket = _t5_causal_bucket(jnp.clip(m, 0, STEPS) * dilation)
    table = rel_bias[:, g * HEADS_PER_GROUP:(g + 1) * HEADS_PER_GROUP].astype(F32)
    bias = table[bucket].transpose(2, 0, 1)
    return jnp.where(band[None], bias, NEG_INF)


def _ab_tail_kernel(glu_ref, o1_ref, o2_ref, o3_ref, l1_ref, l2_ref, l3_ref, h_ref,
                    cw_ref, cb_ref, lng_ref, lnb_ref, wout_ref, mg_ref, wup_ref, wdown_ref,
                    out_ref, hbuf, cat, z_scr):
    @pl.when(pl.program_id(1) == 0)
    def _():
        hbuf[0:CONV_HALO, :] = jnp.zeros((CONV_HALO, CONV_A_CH), F32)

    hbuf[CONV_HALO:CONV_HALO + TM, :] = glu_ref[0]

    first = CONV_HALO - (CONV_A_WIDTH - 1)
    for r0 in range(0, TM, CONV_ROWS):
        acc = jnp.broadcast_to(cb_ref[...], (CONV_ROWS, CONV_A_CH))
        for k in range(CONV_A_WIDTH):
            acc = acc + cw_ref[k:k + 1, :] * hbuf[r0 + first + k:r0 + first + k + CONV_ROWS, :]
        mu = jnp.mean(acc, axis=-1, keepdims=True)
        cen = acc - mu
        var = jnp.mean(cen * cen, axis=-1, keepdims=True)
        y = cen * lax.rsqrt(var + LN_EPS) * lng_ref[...] + lnb_ref[...]
        cat[r0:r0 + CONV_ROWS, 0:CONV_A_CH] = (y * jax.nn.sigmoid(y)).astype(BF16)

    hbuf[0:CONV_HALO, :] = hbuf[TM:TM + CONV_HALO, :]

    l1, l2, l3 = l1_ref[0], l2_ref[0], l3_ref[0]
    mx = jnp.maximum(jnp.maximum(l1, l2), l3)
    e1, e2, e3 = jnp.exp(l1 - mx), jnp.exp(l2 - mx), jnp.exp(l3 - mx)
    inv = 1.0 / (e1 + e2 + e3)
    head_of_col = lax.broadcasted_iota(jnp.int32, (LANES, ATTN_OUT), 1) // HEAD_DIM
    row = lax.broadcasted_iota(jnp.int32, (LANES, ATTN_OUT), 0)
    spread = jnp.where(head_of_col == row, 1.0, 0.0).astype(BF16)

    def expand(w):
        hi = w.astype(BF16)
        lo = (w - hi.astype(F32)).astype(BF16)
        return _dot(hi, spread) + _dot(lo, spread)

    yb = (expand(e1 * inv) * o1_ref[0].astype(F32)
          + expand(e2 * inv) * o2_ref[0].astype(F32)
          + expand(e3 * inv) * o3_ref[0].astype(F32))
    cat[:, CONV_A_CH:] = yb.astype(BF16)

    h1 = h_ref[0] + _dot(cat[...], wout_ref[...])
    out_ref[0] = _mlp(h1, mg_ref, wup_ref, wdown_ref, z_scr)


def _ab_tail(glu, outs, lses, h, cw, cb, lng, lnb, wout, mg, wup, wdown):
    bsz, seq, d = h.shape

    def tile(width):
        return pl.BlockSpec((1, TM, width), lambda b, s: (b, s, 0))

    def whole(a):
        return pl.BlockSpec(a.shape, lambda b, s: (0,) * a.ndim)

    return pl.pallas_call(
        _ab_tail_kernel,
        grid=(bsz, seq // TM),
        in_specs=[tile(CONV_A_CH)] + [tile(ATTN_OUT)] * 3 + [tile(LANES)] * 3 + [tile(d)]
        + [whole(a) for a in (cw, cb, lng, lnb, wout, mg, wup, wdown)],
        out_specs=tile(d),
        out_shape=jax.ShapeDtypeStruct((bsz, seq, d), F32),
        scratch_shapes=[
            pltpu.VMEM((CONV_HALO + TM, CONV_A_CH), F32),
            pltpu.VMEM((TM, 2 * CONV_A_CH), BF16),
            pltpu.VMEM((TM, D_FF), BF16),
        ],
        compiler_params=pltpu.CompilerParams(
            dimension_semantics=("arbitrary", "arbitrary"),
            vmem_limit_bytes=VMEM_LIMIT),
        name="ab_tail",
    )(glu, *outs, *lses, h, cw, cb, lng, lnb, wout, mg, wup, wdown)


def _sc_layer_kernel(h_ref, ng_ref, win_ref, cw_ref, wout_ref, mg_ref, wup_ref, wdown_ref,
                     fg_ref, out_ref, cvbuf, z_scr):
    halo = 8

    @pl.when(pl.program_id(1) == 0)
    def _():
        cvbuf[0:halo, :] = jnp.zeros((halo, D_MODEL), F32)

    h = h_ref[0]
    xn = _rms_norm(h, ng_ref[...]).astype(BF16)
    c_gate = _dot(xn, win_ref[:, D_MODEL:2 * D_MODEL])
    val = _dot(xn, win_ref[:, 2 * D_MODEL:])
    cvbuf[halo:halo + TM, :] = c_gate * val
    first = halo - (SC_CONV_WIDTH - 1)
    conv = cw_ref[0:1, :] * cvbuf[first:first + TM, :]
    for k in range(1, SC_CONV_WIDTH):
        conv = conv + cw_ref[k:k + 1, :] * cvbuf[first + k:first + k + TM, :]
    cvbuf[0:halo, :] = cvbuf[TM:TM + halo, :]
    b_gate = _dot(xn, win_ref[:, :D_MODEL])
    h1 = h + _dot((b_gate * conv).astype(BF16), wout_ref[...])
    h2 = _mlp(h1, mg_ref, wup_ref, wdown_ref, z_scr)
    out_ref[0] = _rms_norm(h2, fg_ref[...])


def _sc_layer(h, ng, win, cw, wout, mg, wup, wdown, fg):
    bsz, seq, d = h.shape

    def whole(a):
        return pl.BlockSpec(a.shape, lambda b, s: (0,) * a.ndim)

    tile = pl.BlockSpec((1, TM, d), lambda b, s: (b, s, 0))
    return pl.pallas_call(
        _sc_layer_kernel,
        grid=(bsz, seq // TM),
        in_specs=[tile] + [whole(a) for a in (ng, win, cw, wout, mg, wup, wdown, fg)],
        out_specs=tile,
        out_shape=jax.ShapeDtypeStruct((bsz, seq, d), F32),
        scratch_shapes=[
            pltpu.VMEM((8 + TM, d), F32),
            pltpu.VMEM((TM, D_FF), BF16),
        ],
        compiler_params=pltpu.CompilerParams(
            dimension_semantics=("arbitrary", "arbitrary"),
            vmem_limit_bytes=VMEM_LIMIT),
        name="sc_layer",
    )(h, ng, win, cw, wout, mg, wup, wdown, fg)


def kernel(x, rel_bias, ab_norm, ab_w_in, ab_conv_w, ab_conv_b, ab_ln_g, ab_ln_b, ab_w_out,
           sc_norm, sc_w_in, sc_conv_w, sc_w_out, mlp_norm, mlp_w_up, mlp_w_down, final_norm):
    assert x.shape[1] % (TJ * DILATED_GROUPS[-1][1]) == 0 and x.shape[2] == D_MODEL
    assert ab_norm.shape[0] == 1 and sc_norm.shape[0] == 1 and mlp_norm.shape[0] == 2

    def row(v):
        return v.reshape(1, -1).astype(F32)

    glu, qkv = _inproj_ab(x, row(ab_norm[0]), ab_w_in[0].astype(BF16))
    outs, lses = [], []
    for g, (_, dilation) in enumerate(DILATED_GROUPS):
        o, lse = _attention_group(qkv, _group_bias(rel_bias, g, dilation), g, dilation)
        outs.append(o)
        lses.append(lse)
    h = _ab_tail(glu, outs, lses, x, ab_conv_w[0].astype(F32), row(ab_conv_b[0]),
                 row(ab_ln_g[0]), row(ab_ln_b[0]), ab_w_out[0].astype(BF16),
                 row(mlp_norm[0]), mlp_w_up[0].astype(BF16), mlp_w_down[0].astype(BF16))
    return _sc_layer(h, row(sc_norm[0]), sc_w_in[0].astype(BF16), sc_conv_w[0].astype(F32),
                     sc_w_out[0].astype(BF16), row(mlp_norm[1]), mlp_w_up[1].astype(BF16),
                     mlp_w_down[1].astype(BF16), row(final_norm))
```

```python
import functools
import math

import jax
import jax.numpy as jnp
from jax import lax
from jax.experimental import pallas as pl
from jax.experimental.pallas import tpu as pltpu

D_MODEL = 1024
CONV_A_CH = 512
CONV_A_WIDTH = 31
LN_EPS = 1e-5
HEAD_DIM = 64
HEADS_PER_GROUP = 8
DILATED_GROUPS = ((128, 1), (512, 4), (2048, 16))
N_GROUPS = 3
ATTN_OUT = HEADS_PER_GROUP * HEAD_DIM
NUM_BUCKETS = 32
REL_MAX_DISTANCE = 2048
SC_CONV_WIDTH = 3
D_FF = 4 * D_MODEL
RMS_EPS = 1e-6
NEG_INF = -1e30

STEPS = 128
LANES = 128
SUBLANES = 8
VMEM_LIMIT = 56 * 1024 * 1024

TM = 512
TJ = 512
CONV_HALO = 32
CONV_ROWS = 64
FF_CHUNK = 1024
SLABS = ATTN_OUT // LANES

BF16 = jnp.bfloat16
F32 = jnp.float32


def _dot(a, b):
    return jnp.dot(a, b, preferred_element_type=F32)


def _dot_nt(a, b):
    return lax.dot_general(a, b, (((1,), (1,)), ((), ())), preferred_element_type=F32)


def _rms_norm(x, g):
    y = x * lax.rsqrt(jnp.mean(x * x, axis=-1, keepdims=True) + RMS_EPS)
    return y * g


def _mlp(h, g_ref, wup_ref, wdown_ref, z_scr):
    xn = _rms_norm(h, g_ref[...]).astype(BF16)
    for c in range(D_FF // FF_CHUNK):
        cols = slice(c * FF_CHUNK, (c + 1) * FF_CHUNK)
        z = jnp.maximum(_dot(xn, wup_ref[:, cols]), 0.0)
        z_scr[:, cols] = (z * z).astype(BF16)
    return h + _dot(z_scr[...], wdown_ref[...])


def _conv_module(hbuf, cw_ref, cb_ref, lng_ref, lnb_ref, ya_ref):
    lead = CONV_HALO - (CONV_A_WIDTH - 1)
    for r0 in range(0, TM, CONV_ROWS):
        acc = jnp.broadcast_to(cb_ref[...], (CONV_ROWS, CONV_A_CH))
        for b in range(SUBLANES):
            ext = CONV_ROWS + (SUBLANES if b else 0)
            part = None
            for a in range((lead + CONV_A_WIDTH - 1) // SUBLANES + 1):
                k = SUBLANES * a + b - lead
                if 0 <= k < CONV_A_WIDTH:
                    term = cw_ref[k:k + 1, :] * hbuf[r0 + SUBLANES * a:r0 + SUBLANES * a + ext, :]
                    part = term if part is None else part + term
            acc = acc + part[b:b + CONV_ROWS, :]
        mu = jnp.mean(acc, axis=-1, keepdims=True)
        cen = acc - mu
        var = jnp.mean(cen * cen, axis=-1, keepdims=True)
        y = cen * lax.rsqrt(var + LN_EPS) * lng_ref[...] + lnb_ref[...]
        ya_ref[0, r0:r0 + CONV_ROWS, :] = (y * jax.nn.sigmoid(y)).astype(BF16)


def _inproj_ab_kernel(x_ref, g_ref, w_ref, cw_ref, cb_ref, lng_ref, lnb_ref,
                      ya_ref, qkv1_ref, qkv2_ref, qkv3_ref, hbuf, slab):
    @pl.when(pl.program_id(1) == 0)
    def _():
        hbuf[0:CONV_HALO, :] = jnp.zeros((CONV_HALO, CONV_A_CH), F32)

    xn = _rms_norm(x_ref[0], g_ref[...]).astype(BF16)
    val = _dot(xn, w_ref[:, :CONV_A_CH])
    gate = _dot(xn, w_ref[:, CONV_A_CH:2 * CONV_A_CH])
    hbuf[CONV_HALO:CONV_HALO + TM, :] = val * jax.nn.sigmoid(gate)
    _conv_module(hbuf, cw_ref, cb_ref, lng_ref, lnb_ref, ya_ref)
    hbuf[0:CONV_HALO, :] = hbuf[TM:TM + CONV_HALO, :]

    outs = (qkv1_ref, qkv2_ref, qkv3_ref)
    n = 0
    for which in range(3):
        for g, (_, dilation) in enumerate(DILATED_GROUPS):
            lo = 2 * CONV_A_CH + (which * N_GROUPS + g) * ATTN_OUT
            acc = _dot(xn, w_ref[:, lo:lo + ATTN_OUT])
            if which == 0:
                acc = acc * (HEAD_DIM ** -0.5)
            dst = which * ATTN_OUT
            if dilation == 1:
                outs[g][0, 0, :, dst:dst + ATTN_OUT] = acc.astype(BF16)
                continue
            for s in range(SLABS):
                slab[n, s] = acc[:, s * LANES:(s + 1) * LANES]
            rows = TM // dilation
            for r in range(dilation):
                for s in range(SLABS):
                    piece = slab[n, s, pl.ds(r, rows, stride=dilation), :]
                    outs[g][0, r, :, dst + s * LANES:dst + (s + 1) * LANES] = piece.astype(BF16)
            n += 1


def _inproj_ab(h, g, w, cw, cb, lng, lnb):
    bsz, seq, d = h.shape

    def whole(a):
        return pl.BlockSpec(a.shape, lambda b, s: (0,) * a.ndim)

    n_strided = 3 * sum(1 for _, dil in DILATED_GROUPS if dil > 1)
    return pl.pallas_call(
        _inproj_ab_kernel,
        grid=(bsz, seq // TM),
        in_specs=[pl.BlockSpec((1, TM, d), lambda b, s: (b, s, 0))]
        + [whole(a) for a in (g, w, cw, cb, lng, lnb)],
        out_specs=[pl.BlockSpec((1, TM, CONV_A_CH), lambda b, s: (b, s, 0))]
        + [pl.BlockSpec((1, dil, TM // dil, 3 * ATTN_OUT), lambda b, s: (b, 0, s, 0))
           for _, dil in DILATED_GROUPS],
        out_shape=[jax.ShapeDtypeStruct((bsz, seq, CONV_A_CH), BF16)]
        + [jax.ShapeDtypeStruct((bsz, dil, seq // dil, 3 * ATTN_OUT), BF16)
           for _, dil in DILATED_GROUPS],
        scratch_shapes=[
            pltpu.VMEM((CONV_HALO + TM, CONV_A_CH), F32),
            pltpu.VMEM((n_strided, SLABS, TM, LANES), F32),
        ],
        compiler_params=pltpu.CompilerParams(
            dimension_semantics=("arbitrary", "arbitrary"),
            vmem_limit_bytes=VMEM_LIMIT),
        name="inproj_ab",
    )(h, g, w, cw, cb, lng, lnb)


def _attn_kernel(bucket_ref, rel_ref, q_ref, k_ref, kp_ref, v_ref, vp_ref, o_ref, lse_ref,
                 bias_scr, *, head0):
    pair = 2 * STEPS

    @pl.when((pl.program_id(0) == 0) & (pl.program_id(1) == 0) & (pl.program_id(2) == 0))
    def _():
        bucket = bucket_ref[...]
        for h in range(HEADS_PER_GROUP):
            acc = jnp.full((STEPS, pair), NEG_INF, F32)
            for b in range(NUM_BUCKETS):
                acc = jnp.where(bucket == b, rel_ref[b, head0 + h], acc)
            bias_scr[h * STEPS:(h + 1) * STEPS, :] = acc

    first_tile = pl.program_id(2) == 0
    lane = lax.broadcasted_iota(jnp.int32, (STEPS, LANES), 1)
    keep_lo = jnp.where(lax.broadcasted_iota(jnp.int32, (1, LANES), 1) < HEAD_DIM, 1.0, 0.0).astype(BF16)
    keep_hi = jnp.where(lax.broadcasted_iota(jnp.int32, (1, LANES), 1) < HEAD_DIM, 0.0, 1.0).astype(BF16)
    ones = jnp.ones((pair, LANES), BF16)
    for i in range(TJ // STEPS):
        rows = slice(i * STEPS, (i + 1) * STEPS)
        prev = slice((i - 1) * STEPS, i * STEPS)
        scores = []
        for j in range(HEADS_PER_GROUP // 2):
            cols = slice(j * LANES, (j + 1) * LANES)
            qp = q_ref[0, 0, rows, cols]
            q2 = jnp.concatenate([qp * keep_lo, qp * keep_hi], axis=0)
            kprev = kp_ref[0, 0, :, cols] if i == 0 else k_ref[0, 0, prev, cols]
            k2 = jnp.concatenate([kprev, k_ref[0, 0, rows, cols]], axis=0)
            scores.append(_dot_nt(q2, k2) + bias_scr[j * pair:(j + 1) * pair, :])
        probs, maxes = [], []
        for j in range(HEADS_PER_GROUP // 2):
            sp, sc = scores[j][:, :STEPS], scores[j][:, STEPS:]
            if i == 0:
                sp = jnp.where(first_tile, NEG_INF, sp)
            mx = jnp.max(jnp.maximum(sp, sc), axis=-1, keepdims=True)
            probs.append(jnp.concatenate([jnp.exp(sp - mx), jnp.exp(sc - mx)], axis=1).astype(BF16))
            maxes.append(mx)
        tile_m = jnp.zeros((STEPS, LANES), F32)
        tile_den = jnp.ones((STEPS, LANES), F32)
        for j in range(HEADS_PER_GROUP // 2):
            cols = slice(j * LANES, (j + 1) * LANES)
            vprev = vp_ref[0, 0, :, cols] if i == 0 else v_ref[0, 0, prev, cols]
            v2 = jnp.concatenate([vprev, v_ref[0, 0, rows, cols]], axis=0)
            o = _dot(probs[j], jnp.concatenate([v2, ones], axis=1))
            den = o[:, LANES:]
            on = o[:, :LANES] / den
            o_ref[0, 0, rows, cols] = jnp.where(lane < HEAD_DIM, on[:STEPS], on[STEPS:]).astype(BF16)
            for e in range(2):
                part = slice(e * STEPS, (e + 1) * STEPS)
                tile_m = jnp.where(lane == 2 * j + e, maxes[j][part], tile_m)
                tile_den = jnp.where(lane == 2 * j + e, den[part], tile_den)
        lse_ref[0, 0, rows, :] = tile_m + jnp.log(tile_den)


def _attention_group(qkv, bucket, rel_bias, g):
    bsz, dilation, sd, _ = qkv.shape
    per_tile = TJ // STEPS

    def cur(col):
        return pl.BlockSpec((1, 1, TJ, ATTN_OUT), lambda b, r, t: (b, r, t, col))

    def prev(col):
        return pl.BlockSpec((1, 1, STEPS, ATTN_OUT),
                            lambda b, r, t: (b, r, jnp.maximum(t * per_tile - 1, 0), col))

    return pl.pallas_call(
        functools.partial(_attn_kernel, head0=g * HEADS_PER_GROUP),
        grid=(bsz, dilation, sd // TJ),
        in_specs=[
            pl.BlockSpec(bucket.shape, lambda b, r, t: (0, 0)),
            pl.BlockSpec(memory_space=pltpu.SMEM),
            cur(0), cur(1), prev(1), cur(2), prev(2),
        ],
        out_specs=[
            pl.BlockSpec((1, 1, TJ, ATTN_OUT), lambda b, r, t: (b, r, t, 0)),
            pl.BlockSpec((1, 1, TJ, LANES), lambda b, r, t: (b, r, t, 0)),
        ],
        out_shape=[
            jax.ShapeDtypeStruct((bsz, dilation, sd, ATTN_OUT), BF16),
            jax.ShapeDtypeStruct((bsz, dilation, sd, LANES), F32),
        ],
        scratch_shapes=[pltpu.VMEM((HEADS_PER_GROUP * STEPS, 2 * STEPS), F32)],
        compiler_params=pltpu.CompilerParams(
            dimension_semantics=("arbitrary", "arbitrary", "arbitrary"),
            vmem_limit_bytes=VMEM_LIMIT),
        name=f"attn_d{dilation}",
    )(bucket, rel_bias, qkv, qkv, qkv, qkv, qkv)


def _t5_causal_bucket(n):
    max_exact = NUM_BUCKETS // 2
    nf = jnp.maximum(n, 1).astype(F32)
    large = max_exact + (jnp.log(nf / max_exact) / math.log(REL_MAX_DISTANCE / max_exact)
                         * (NUM_BUCKETS - max_exact)).astype(jnp.int32)
    return jnp.where(n < max_exact, n, jnp.minimum(large, NUM_BUCKETS - 1))


def _band_buckets(dilation):
    a_idx = jnp.arange(STEPS)[:, None]
    c_idx = jnp.arange(2 * STEPS)[None, :]
    m = a_idx + STEPS - c_idx
    band = (m >= 0) & (m <= STEPS)
    return jnp.where(band, _t5_causal_bucket(jnp.clip(m, 0, STEPS) * dilation), -1)


def _ab_tail_kernel(ya_ref, o1_ref, o2_ref, o3_ref, l1_ref, l2_ref, l3_ref, h_ref,
                    wout_ref, mg_ref, wup_ref, wdown_ref, out_ref, onat, lnat, z_scr):
    strided = [(o_ref, l_ref, dil) for (o_ref, l_ref, (_, dil))
               in zip((o1_ref, o2_ref, o3_ref), (l1_ref, l2_ref, l3_ref), DILATED_GROUPS) if dil > 1]
    for n, (o_ref, l_ref, dil) in enumerate(strided):
        rows = TM // dil
        for r in range(dil):
            lnat[n, pl.ds(r, rows, stride=dil), :] = l_ref[0, r]
            for s in range(SLABS):
                onat[n, s, pl.ds(r, rows, stride=dil), :] = (
                    o_ref[0, r, :, s * LANES:(s + 1) * LANES].astype(F32))

    outs, lses, n = [], [], 0
    for o_ref, l_ref, (_, dil) in zip((o1_ref, o2_ref, o3_ref), (l1_ref, l2_ref, l3_ref),
                                      DILATED_GROUPS):
        if dil == 1:
            outs.append(o_ref[0, 0].astype(F32))
            lses.append(l_ref[0, 0])
        else:
            outs.append(jnp.concatenate([onat[n, s] for s in range(SLABS)], axis=1))
            lses.append(lnat[n])
            n += 1

    mx = jnp.maximum(jnp.maximum(lses[0], lses[1]), lses[2])
    es = [jnp.exp(l - mx) for l in lses]
    inv = 1.0 / (es[0] + es[1] + es[2])
    head_of_col = lax.broadcasted_iota(jnp.int32, (LANES, ATTN_OUT), 1) // HEAD_DIM
    row = lax.broadcasted_iota(jnp.int32, (LANES, ATTN_OUT), 0)
    spread = jnp.where(head_of_col == row, 1.0, 0.0).astype(BF16)

    def expand(w):
        hi = w.astype(BF16)
        lo = (w - hi.astype(F32)).astype(BF16)
        return _dot(hi, spread) + _dot(lo, spread)

    yb = (expand(es[0] * inv) * outs[0] + expand(es[1] * inv) * outs[1]
          + expand(es[2] * inv) * outs[2])
    y = _dot(ya_ref[0], wout_ref[:CONV_A_CH, :]) + _dot(yb.astype(BF16), wout_ref[CONV_A_CH:, :])
    out_ref[0] = _mlp(h_ref[0] + y, mg_ref, wup_ref, wdown_ref, z_scr)


def _ab_tail(ya, outs, lses, h, wout, mg, wup, wdown):
    bsz, seq, d = h.shape

    def tile(width):
        return pl.BlockSpec((1, TM, width), lambda b, s: (b, s, 0))

    def residue_tile(dil, width):
        return pl.BlockSpec((1, dil, TM // dil, width), lambda b, s: (b, 0, s, 0))

    def whole(a):
        return pl.BlockSpec(a.shape, lambda b, s: (0,) * a.ndim)

    n_strided = sum(1 for _, dil in DILATED_GROUPS if dil > 1)
    return pl.pallas_call(
        _ab_tail_kernel,
        grid=(bsz, seq // TM),
        in_specs=[tile(CONV_A_CH)]
        + [residue_tile(dil, ATTN_OUT) for _, dil in DILATED_GROUPS]
        + [residue_tile(dil, LANES) for _, dil in DILATED_GROUPS]
        + [tile(d)] + [whole(a) for a in (wout, mg, wup, wdown)],
        out_specs=tile(d),
        out_shape=jax.ShapeDtypeStruct((bsz, seq, d), F32),
        scratch_shapes=[
            pltpu.VMEM((n_strided, SLABS, TM, LANES), F32),
            pltpu.VMEM((n_strided, TM, LANES), F32),
            pltpu.VMEM((TM, D_FF), BF16),
        ],
        compiler_params=pltpu.CompilerParams(
            dimension_semantics=("arbitrary", "arbitrary"),
            vmem_limit_bytes=VMEM_LIMIT),
        name="ab_tail",
    )(ya, *outs, *lses, h, wout, mg, wup, wdown)


def _sc_layer_kernel(h_ref, ng_ref, win_ref, cw_ref, wout_ref, mg_ref, wup_ref, wdown_ref,
                     fg_ref, out_ref, cvbuf, z_scr):
    halo = SUBLANES

    @pl.when(pl.program_id(1) == 0)
    def _():
        cvbuf[0:halo, :] = jnp.zeros((halo, D_MODEL), F32)

    h = h_ref[0]
    xn = _rms_norm(h, ng_ref[...]).astype(BF16)
    c_gate = _dot(xn, win_ref[:, D_MODEL:2 * D_MODEL])
    val = _dot(xn, win_ref[:, 2 * D_MODEL:])
    cvbuf[halo:halo + TM, :] = c_gate * val
    first = halo - (SC_CONV_WIDTH - 1)
    conv = cw_ref[0:1, :] * cvbuf[first:first + TM, :]
    for k in range(1, SC_CONV_WIDTH):
        conv = conv + cw_ref[k:k + 1, :] * cvbuf[first + k:first + k + TM, :]
    cvbuf[0:halo, :] = cvbuf[TM:TM + halo, :]
    b_gate = _dot(xn, win_ref[:, :D_MODEL])
    h1 = h + _dot((b_gate * conv).astype(BF16), wout_ref[...])
    h2 = _mlp(h1, mg_ref, wup_ref, wdown_ref, z_scr)
    out_ref[0] = _rms_norm(h2, fg_ref[...])


def _sc_layer(h, ng, win, cw, wout, mg, wup, wdown, fg):
    bsz, seq, d = h.shape

    def whole(a):
        return pl.BlockSpec(a.shape, lambda b, s: (0,) * a.ndim)

    tile = pl.BlockSpec((1, TM, d), lambda b, s: (b, s, 0))
    return pl.pallas_call(
        _sc_layer_kernel,
        grid=(bsz, seq // TM),
        in_specs=[tile] + [whole(a) for a in (ng, win, cw, wout, mg, wup, wdown, fg)],
        out_specs=tile,
        out_shape=jax.ShapeDtypeStruct((bsz, seq, d), F32),
        scratch_shapes=[
            pltpu.VMEM((SUBLANES + TM, d), F32),
            pltpu.VMEM((TM, D_FF), BF16),
        ],
        compiler_params=pltpu.CompilerParams(
            dimension_semantics=("arbitrary", "arbitrary"),
            vmem_limit_bytes=VMEM_LIMIT),
        name="sc_layer",
    )(h, ng, win, cw, wout, mg, wup, wdown, fg)


def kernel(x, rel_bias, ab_norm, ab_w_in, ab_conv_w, ab_conv_b, ab_ln_g, ab_ln_b, ab_w_out,
           sc_norm, sc_w_in, sc_conv_w, sc_w_out, mlp_norm, mlp_w_up, mlp_w_down, final_norm):
    assert x.shape[1] % (TJ * DILATED_GROUPS[-1][1]) == 0 and x.shape[2] == D_MODEL
    assert ab_norm.shape[0] == 1 and sc_norm.shape[0] == 1 and mlp_norm.shape[0] == 2

    def row(v):
        return v.reshape(1, -1).astype(F32)

    ya, *qkvs = _inproj_ab(x, row(ab_norm[0]), ab_w_in[0].astype(BF16), ab_conv_w[0].astype(F32),
                           row(ab_conv_b[0]), row(ab_ln_g[0]), row(ab_ln_b[0]))
    outs, lses = [], []
    for g, (_, dilation) in enumerate(DILATED_GROUPS):
        o, lse = _attention_group(qkvs[g], _band_buckets(dilation), rel_bias.astype(F32), g)
        outs.append(o)
        lses.append(lse)
    h = _ab_tail(ya, outs, lses, x, ab_w_out[0].astype(BF16), row(mlp_norm[0]),
                 mlp_w_up[0].astype(BF16), mlp_w_down[0].astype(BF16))
    return _sc_layer(h, row(sc_norm[0]), sc_w_in[0].astype(BF16), sc_conv_w[0].astype(F32),
                     sc_w_out[0].astype(BF16), row(mlp_norm[1]), mlp_w_up[1].astype(BF16),
                     mlp_w_down[1].astype(BF16), row(final_norm))
```

```python
import functools
import math

import jax
import jax.numpy as jnp
from jax import lax
from jax.experimental import pallas as pl
from jax.experimental.pallas import tpu as pltpu

D_MODEL = 1024
CONV_A_CH = 512
CONV_A_WIDTH = 31
LN_EPS = 1e-5
HEAD_DIM = 64
HEADS_PER_GROUP = 8
DILATED_GROUPS = ((128, 1), (512, 4), (2048, 16))
N_GROUPS = 3
ATTN_OUT = HEADS_PER_GROUP * HEAD_DIM
NUM_BUCKETS = 32
REL_MAX_DISTANCE = 2048
SC_CONV_WIDTH = 3
D_FF = 4 * D_MODEL
RMS_EPS = 1e-6
NEG_INF = -1e30

STEPS = 128
LANES = 128
SUBLANES = 8
VMEM_LIMIT = 56 * 1024 * 1024

TM = 512
TJ = 512
CONV_HALO = 32
CONV_ROWS = 64
FF_CHUNK = 1024
SLABS = ATTN_OUT // LANES

BF16 = jnp.bfloat16
F32 = jnp.float32


def _dot(a, b):
    return jnp.dot(a, b, preferred_element_type=F32)


def _dot_nt(a, b):
    return lax.dot_general(a, b, (((1,), (1,)), ((), ())), preferred_element_type=F32)


def _rms_norm(x, g):
    y = x * lax.rsqrt(jnp.mean(x * x, axis=-1, keepdims=True) + RMS_EPS)
    return y * g


def _mlp(h, g_ref, wup_ref, wdown_ref, z_scr):
    xn = _rms_norm(h, g_ref[...]).astype(BF16)
    for c in range(D_FF // FF_CHUNK):
        cols = slice(c * FF_CHUNK, (c + 1) * FF_CHUNK)
        z = jnp.maximum(_dot(xn, wup_ref[:, cols]), 0.0)
        z_scr[:, cols] = (z * z).astype(BF16)
    return h + _dot(z_scr[...], wdown_ref[...])


def _conv_chunk(r0, hbuf, cw_ref, cb_ref, lng_ref, lnb_ref, ya_ref):
    lead = CONV_HALO - (CONV_A_WIDTH - 1)
    accs = []
    for s in range(CONV_A_CH // LANES):
        cols = slice(s * LANES, (s + 1) * LANES)
        acc = jnp.broadcast_to(cb_ref[:, cols], (CONV_ROWS, LANES))
        for k in range(CONV_A_WIDTH):
            acc = acc + cw_ref[k:k + 1, cols] * hbuf[s, r0 + lead + k:r0 + lead + k + CONV_ROWS, :]
        accs.append(acc)
    acc = jnp.concatenate(accs, axis=1)
    mu = jnp.mean(acc, axis=-1, keepdims=True)
    cen = acc - mu
    var = jnp.mean(cen * cen, axis=-1, keepdims=True)
    y = cen * lax.rsqrt(var + LN_EPS) * lng_ref[...] + lnb_ref[...]
    ya_ref[0, r0:r0 + CONV_ROWS, :] = (y * jax.nn.sigmoid(y)).astype(BF16)


def _residue_major_inputs(xf, xn_perm, xslab):
    n_slabs = D_MODEL // LANES
    xn_perm[0] = xf.astype(BF16)
    for s in range(n_slabs):
        xslab[0, s] = xf[:, s * LANES:(s + 1) * LANES]
    residues = [[0]]
    for g in range(1, N_GROUPS):
        prev_dil, dil = DILATED_GROUPS[g - 1][1], DILATED_GROUPS[g][1]
        ratio, rows = dil // prev_dil, TM // dil
        src, dst = (g - 1) % 2, g % 2
        res = []
        for blk, blk_res in enumerate(residues[-1]):
            for r in range(ratio):
                b = blk * ratio + r
                res.append(blk_res + prev_dil * r)
                for s in range(n_slabs):
                    piece = xslab[src, s, pl.ds(blk * (TM // prev_dil) + r, rows, stride=ratio), :]
                    xn_perm[g, b * rows:(b + 1) * rows, s * LANES:(s + 1) * LANES] = piece.astype(BF16)
                    if g + 1 < N_GROUPS:
                        xslab[dst, s, b * rows:(b + 1) * rows, :] = piece
        residues.append(res)
    return residues


def _qkv_chunk(xn_perm, w_ref, which, g, residues, out_ref):
    lo = 2 * CONV_A_CH + (which * N_GROUPS + g) * ATTN_OUT
    acc = _dot(xn_perm[g], w_ref[:, lo:lo + ATTN_OUT]).astype(BF16)
    rows = TM // len(residues)
    for b, r in enumerate(residues):
        out_ref[0, r, :, which * ATTN_OUT:(which + 1) * ATTN_OUT] = acc[b * rows:(b + 1) * rows]


def _inproj_ab_kernel(x_ref, g_ref, w_ref, cw_ref, cb_ref, lng_ref, lnb_ref,
                      ya_ref, qkv1_ref, qkv2_ref, qkv3_ref, hbuf, xn_perm, xslab):
    @pl.when(pl.program_id(1) == 0)
    def _():
        hbuf[:, 0:CONV_HALO, :] = jnp.zeros((CONV_A_CH // LANES, CONV_HALO, LANES), F32)

    xf = _rms_norm(x_ref[0], g_ref[...])
    xn = xf.astype(BF16)
    val = _dot(xn, w_ref[:, :CONV_A_CH])
    gate = _dot(xn, w_ref[:, CONV_A_CH:2 * CONV_A_CH])
    glu = val * jax.nn.sigmoid(gate)
    for s in range(CONV_A_CH // LANES):
        hbuf[s, CONV_HALO:CONV_HALO + TM, :] = glu[:, s * LANES:(s + 1) * LANES]
    residues = _residue_major_inputs(xf, xn_perm, xslab)

    outs = (qkv1_ref, qkv2_ref, qkv3_ref)
    conv_rows = list(range(0, TM, CONV_ROWS))
    blocks = [(which, g) for which in range(3) for g in range(N_GROUPS)]
    always = (pl.program_id(0) >= 0, pl.program_id(1) >= 0)
    for idx in range(max(len(blocks), len(conv_rows))):
        def stage(idx=idx):
            if idx < len(blocks):
                which, g = blocks[idx]
                _qkv_chunk(xn_perm, w_ref, which, g, residues[g], outs[g])
            if idx < len(conv_rows):
                _conv_chunk(conv_rows[idx], hbuf, cw_ref, cb_ref, lng_ref, lnb_ref, ya_ref)
        pl.when(always[idx % 2])(stage)
    hbuf[:, 0:CONV_HALO, :] = hbuf[:, TM:TM + CONV_HALO, :]


def _inproj_ab(h, g, w, cw, cb, lng, lnb):
    bsz, seq, d = h.shape

    def whole(a):
        return pl.BlockSpec(a.shape, lambda b, s: (0,) * a.ndim)

    for (_, prev_dil), (_, dil) in zip(DILATED_GROUPS, DILATED_GROUPS[1:]):
        assert dil % prev_dil == 0 and TM % (2 * SUBLANES * dil) == 0
    assert DILATED_GROUPS[0][1] == 1
    return pl.pallas_call(
        _inproj_ab_kernel,
        grid=(bsz, seq // TM),
        in_specs=[pl.BlockSpec((1, TM, d), lambda b, s: (b, s, 0))]
        + [whole(a) for a in (g, w, cw, cb, lng, lnb)],
        out_specs=[pl.BlockSpec((1, TM, CONV_A_CH), lambda b, s: (b, s, 0))]
        + [pl.BlockSpec((1, dil, TM // dil, 3 * ATTN_OUT), lambda b, s: (b, 0, s, 0))
           for _, dil in DILATED_GROUPS],
        out_shape=[jax.ShapeDtypeStruct((bsz, seq, CONV_A_CH), BF16)]
        + [jax.ShapeDtypeStruct((bsz, dil, seq // dil, 3 * ATTN_OUT), BF16)
           for _, dil in DILATED_GROUPS],
        scratch_shapes=[
            pltpu.VMEM((CONV_A_CH // LANES, CONV_HALO + TM, LANES), F32),
            pltpu.VMEM((N_GROUPS, TM, D_MODEL), BF16),
            pltpu.VMEM((2, D_MODEL // LANES, TM, LANES), F32),
        ],
        compiler_params=pltpu.CompilerParams(
            dimension_semantics=("arbitrary", "arbitrary"),
            vmem_limit_bytes=VMEM_LIMIT),
        name="inproj_ab",
    )(h, g, w, cw, cb, lng, lnb)


def _attn_kernel(bucket_ref, rel_ref, q_ref, k_ref, kp_ref, v_ref, vp_ref, o_ref, lse_ref,
                 bias_scr, *, head0):
    pair = 2 * STEPS

    @pl.when((pl.program_id(0) == 0) & (pl.program_id(1) == 0) & (pl.program_id(2) == 0))
    def _():
        bucket = bucket_ref[...]
        for h in range(HEADS_PER_GROUP):
            acc = jnp.full((STEPS, pair), NEG_INF, F32)
            for b in range(NUM_BUCKETS):
                acc = jnp.where(bucket == b, rel_ref[b, head0 + h], acc)
            bias_scr[h * STEPS:(h + 1) * STEPS, :] = acc

    first_tile = pl.program_id(2) == 0
    lane = lax.broadcasted_iota(jnp.int32, (STEPS, LANES), 1)
    keep_lo = jnp.where(lax.broadcasted_iota(jnp.int32, (1, LANES), 1) < HEAD_DIM, 1.0, 0.0).astype(BF16)
    keep_hi = jnp.where(lax.broadcasted_iota(jnp.int32, (1, LANES), 1) < HEAD_DIM, 0.0, 1.0).astype(BF16)
    ones = jnp.ones((pair, LANES), BF16)
    for i in range(TJ // STEPS):
        rows = slice(i * STEPS, (i + 1) * STEPS)
        prev = slice((i - 1) * STEPS, i * STEPS)
        scores = []
        for j in range(HEADS_PER_GROUP // 2):
            cols = slice(j * LANES, (j + 1) * LANES)
            qp = q_ref[0, 0, rows, cols]
            q2 = jnp.concatenate([qp * keep_lo, qp * keep_hi], axis=0)
            kprev = kp_ref[0, 0, :, cols] if i == 0 else k_ref[0, 0, prev, cols]
            k2 = jnp.concatenate([kprev, k_ref[0, 0, rows, cols]], axis=0)
            scores.append(_dot_nt(q2, k2) + bias_scr[j * pair:(j + 1) * pair, :])
        probs, maxes = [], []
        for j in range(HEADS_PER_GROUP // 2):
            sp, sc = scores[j][:, :STEPS], scores[j][:, STEPS:]
            if i == 0:
                sp = jnp.where(first_tile, NEG_INF, sp)
            mx = jnp.max(jnp.maximum(sp, sc), axis=-1, keepdims=True)
            probs.append(jnp.concatenate([jnp.exp(sp - mx), jnp.exp(sc - mx)], axis=1).astype(BF16))
            maxes.append(mx)
        tile_m = jnp.zeros((STEPS, LANES), F32)
        tile_den = jnp.ones((STEPS, LANES), F32)
        for j in range(HEADS_PER_GROUP // 2):
            cols = slice(j * LANES, (j + 1) * LANES)
            vprev = vp_ref[0, 0, :, cols] if i == 0 else v_ref[0, 0, prev, cols]
            v2 = jnp.concatenate([vprev, v_ref[0, 0, rows, cols]], axis=0)
            o = _dot(probs[j], jnp.concatenate([v2, ones], axis=1))
            den = o[:, LANES:]
            on = o[:, :LANES] / den
            o_ref[0, 0, rows, cols] = jnp.where(lane < HEAD_DIM, on[:STEPS], on[STEPS:]).astype(BF16)
            for e in range(2):
                part = slice(e * STEPS, (e + 1) * STEPS)
                tile_m = jnp.where(lane == 2 * j + e, maxes[j][part], tile_m)
                tile_den = jnp.where(lane == 2 * j + e, den[part], tile_den)
        lse_ref[0, 0, rows, :] = tile_m + jnp.log(tile_den)


def _attention_group(qkv, bucket, rel_bias, g):
    bsz, dilation, sd, _ = qkv.shape
    per_tile = TJ // STEPS

    def cur(col):
        return pl.BlockSpec((1, 1, TJ, ATTN_OUT), lambda b, r, t: (b, r, t, col))

    def prev(col):
        return pl.BlockSpec((1, 1, STEPS, ATTN_OUT),
                            lambda b, r, t: (b, r, jnp.maximum(t * per_tile - 1, 0), col))

    return pl.pallas_call(
        functools.partial(_attn_kernel, head0=g * HEADS_PER_GROUP),
        grid=(bsz, dilation, sd // TJ),
        in_specs=[
            pl.BlockSpec(bucket.shape, lambda b, r, t: (0, 0)),
            pl.BlockSpec(memory_space=pltpu.SMEM),
            cur(0), cur(1), prev(1), cur(2), prev(2),
        ],
        out_specs=[
            pl.BlockSpec((1, 1, TJ, ATTN_OUT), lambda b, r, t: (b, r, t, 0)),
            pl.BlockSpec((1, 1, TJ, LANES), lambda b, r, t: (b, r, t, 0)),
        ],
        out_shape=[
            jax.ShapeDtypeStruct((bsz, dilation, sd, ATTN_OUT), BF16),
            jax.ShapeDtypeStruct((bsz, dilation, sd, LANES), F32),
        ],
        scratch_shapes=[pltpu.VMEM((HEADS_PER_GROUP * STEPS, 2 * STEPS), F32)],
        compiler_params=pltpu.CompilerParams(
            dimension_semantics=("arbitrary", "arbitrary", "arbitrary"),
            vmem_limit_bytes=VMEM_LIMIT),
        name=f"attn_d{dilation}",
    )(bucket, rel_bias, qkv, qkv, qkv, qkv, qkv)


def _t5_causal_bucket(n):
    max_exact = NUM_BUCKETS // 2
    nf = jnp.maximum(n, 1).astype(F32)
    large = max_exact + (jnp.log(nf / max_exact) / math.log(REL_MAX_DISTANCE / max_exact)
                         * (NUM_BUCKETS - max_exact)).astype(jnp.int32)
    return jnp.where(n < max_exact, n, jnp.minimum(large, NUM_BUCKETS - 1))


def _band_buckets(dilation):
    a_idx = jnp.arange(STEPS)[:, None]
    c_idx = jnp.arange(2 * STEPS)[None, :]
    m = a_idx + STEPS - c_idx
    band = (m >= 0) & (m <= STEPS)
    return jnp.where(band, _t5_causal_bucket(jnp.clip(m, 0, STEPS) * dilation), -1)


def _ab_tail_kernel(ya_ref, o1_ref, o2_ref, o3_ref, l1_ref, l2_ref, l3_ref, h_ref,
                    wout_ref, mg_ref, wup_ref, wdown_ref, out_ref, onat, lnat, z_scr):
    strided = [(o_ref, l_ref, dil) for (o_ref, l_ref, (_, dil))
               in zip((o1_ref, o2_ref, o3_ref), (l1_ref, l2_ref, l3_ref), DILATED_GROUPS) if dil > 1]
    for n, (o_ref, l_ref, dil) in enumerate(strided):
        rows = TM // dil
        for r in range(dil):
            lnat[n, pl.ds(r, rows, stride=dil), :] = l_ref[0, r]
            for s in range(SLABS):
                onat[n, s, pl.ds(r, rows, stride=dil), :] = (
                    o_ref[0, r, :, s * LANES:(s + 1) * LANES].astype(F32))

    outs, lses, n = [], [], 0
    for o_ref, l_ref, (_, dil) in zip((o1_ref, o2_ref, o3_ref), (l1_ref, l2_ref, l3_ref),
                                      DILATED_GROUPS):
        if dil == 1:
            outs.append(o_ref[0, 0].astype(F32))
            lses.append(l_ref[0, 0])
        else:
            outs.append(jnp.concatenate([onat[n, s] for s in range(SLABS)], axis=1))
            lses.append(lnat[n])
            n += 1

    mx = jnp.maximum(jnp.maximum(lses[0], lses[1]), lses[2])
    es = [jnp.exp(l - mx) for l in lses]
    inv = 1.0 / (es[0] + es[1] + es[2])
    head_of_col = lax.broadcasted_iota(jnp.int32, (LANES, ATTN_OUT), 1) // HEAD_DIM
    row = lax.broadcasted_iota(jnp.int32, (LANES, ATTN_OUT), 0)
    spread = jnp.where(head_of_col == row, 1.0, 0.0).astype(BF16)

    def expand(w):
        hi = w.astype(BF16)
        lo = (w - hi.astype(F32)).astype(BF16)
        return _dot(hi, spread) + _dot(lo, spread)

    yb = (expand(es[0] * inv) * outs[0] + expand(es[1] * inv) * outs[1]
          + expand(es[2] * inv) * outs[2])
    y = _dot(ya_ref[0], wout_ref[:CONV_A_CH, :]) + _dot(yb.astype(BF16), wout_ref[CONV_A_CH:, :])
    out_ref[0] = _mlp(h_ref[0] + y, mg_ref, wup_ref, wdown_ref, z_scr)


def _ab_tail(ya, outs, lses, h, wout, mg, wup, wdown):
    bsz, seq, d = h.shape

    def tile(width):
        return pl.BlockSpec((1, TM, width), lambda b, s: (b, s, 0))

    def residue_tile(dil, width):
        return pl.BlockSpec((1, dil, TM // dil, width), lambda b, s: (b, 0, s, 0))

    def whole(a):
        return pl.BlockSpec(a.shape, lambda b, s: (0,) * a.ndim)

    n_strided = sum(1 for _, dil in DILATED_GROUPS if dil > 1)
    return pl.pallas_call(
        _ab_tail_kernel,
        grid=(bsz, seq // TM),
        in_specs=[tile(CONV_A_CH)]
        + [residue_tile(dil, ATTN_OUT) for _, dil in DILATED_GROUPS]
        + [residue_tile(dil, LANES) for _, dil in DILATED_GROUPS]
        + [tile(d)] + [whole(a) for a in (wout, mg, wup, wdown)],
        out_specs=tile(d),
        out_shape=jax.ShapeDtypeStruct((bsz, seq, d), F32),
        scratch_shapes=[
            pltpu.VMEM((n_strided, SLABS, TM, LANES), F32),
            pltpu.VMEM((n_strided, TM, LANES), F32),
            pltpu.VMEM((TM, D_FF), BF16),
        ],
        compiler_params=pltpu.CompilerParams(
            dimension_semantics=("arbitrary", "arbitrary"),
            vmem_limit_bytes=VMEM_LIMIT),
        name="ab_tail",
    )(ya, *outs, *lses, h, wout, mg, wup, wdown)


def _sc_layer_kernel(h_ref, ng_ref, win_ref, cw_ref, wout_ref, mg_ref, wup_ref, wdown_ref,
                     fg_ref, out_ref, cvbuf, z_scr):
    halo = SUBLANES

    @pl.when(pl.program_id(1) == 0)
    def _():
        cvbuf[0:halo, :] = jnp.zeros((halo, D_MODEL), F32)

    h = h_ref[0]
    xn = _rms_norm(h, ng_ref[...]).astype(BF16)
    c_gate = _dot(xn, win_ref[:, D_MODEL:2 * D_MODEL])
    val = _dot(xn, win_ref[:, 2 * D_MODEL:])
    cvbuf[halo:halo + TM, :] = c_gate * val
    first = halo - (SC_CONV_WIDTH - 1)
    conv = cw_ref[0:1, :] * cvbuf[first:first + TM, :]
    for k in range(1, SC_CONV_WIDTH):
        conv = conv + cw_ref[k:k + 1, :] * cvbuf[first + k:first + k + TM, :]
    cvbuf[0:halo, :] = cvbuf[TM:TM + halo, :]
    b_gate = _dot(xn, win_ref[:, :D_MODEL])
    h1 = h + _dot((b_gate * conv).astype(BF16), wout_ref[...])
    h2 = _mlp(h1, mg_ref, wup_ref, wdown_ref, z_scr)
    out_ref[0] = _rms_norm(h2, fg_ref[...])


def _sc_layer(h, ng, win, cw, wout, mg, wup, wdown, fg):
    bsz, seq, d = h.shape

    def whole(a):
        return pl.BlockSpec(a.shape, lambda b, s: (0,) * a.ndim)

    tile = pl.BlockSpec((1, TM, d), lambda b, s: (b, s, 0))
    return pl.pallas_call(
        _sc_layer_kernel,
        grid=(bsz, seq // TM),
        in_specs=[tile] + [whole(a) for a in (ng, win, cw, wout, mg, wup, wdown, fg)],
        out_specs=tile,
        out_shape=jax.ShapeDtypeStruct((bsz, seq, d), F32),
        scratch_shapes=[
            pltpu.VMEM((SUBLANES + TM, d), F32),
            pltpu.VMEM((TM, D_FF), BF16),
        ],
        compiler_params=pltpu.CompilerParams(
            dimension_semantics=("arbitrary", "arbitrary"),
            vmem_limit_bytes=VMEM_LIMIT),
        name="sc_layer",
    )(h, ng, win, cw, wout, mg, wup, wdown, fg)


def kernel(x, rel_bias, ab_norm, ab_w_in, ab_conv_w, ab_conv_b, ab_ln_g, ab_ln_b, ab_w_out,
           sc_norm, sc_w_in, sc_conv_w, sc_w_out, mlp_norm, mlp_w_up, mlp_w_down, final_norm):
    assert x.shape[1] % (TJ * DILATED_GROUPS[-1][1]) == 0 and x.shape[2] == D_MODEL
    assert ab_norm.shape[0] == 1 and sc_norm.shape[0] == 1 and mlp_norm.shape[0] == 2

    def row(v):
        return v.reshape(1, -1).astype(F32)

    q_cols = (jnp.arange(ab_w_in.shape[-1]) >= 2 * CONV_A_CH) & (
        jnp.arange(ab_w_in.shape[-1]) < 2 * CONV_A_CH + N_GROUPS * ATTN_OUT)
    w_in = (ab_w_in[0] * jnp.where(q_cols, HEAD_DIM ** -0.5, 1.0)).astype(BF16)
    ya, *qkvs = _inproj_ab(x, row(ab_norm[0]), w_in, ab_conv_w[0].astype(F32),
                           row(ab_conv_b[0]), row(ab_ln_g[0]), row(ab_ln_b[0]))
    outs, lses = [], []
    for g, (_, dilation) in enumerate(DILATED_GROUPS):
        o, lse = _attention_group(qkvs[g], _band_buckets(dilation), rel_bias.astype(F32), g)
        outs.append(o)
        lses.append(lse)
    h = _ab_tail(ya, outs, lses, x, ab_w_out[0].astype(BF16), row(mlp_norm[0]),
                 mlp_w_up[0].astype(BF16), mlp_w_down[0].astype(BF16))
    return _sc_layer(h, row(sc_norm[0]), sc_w_in[0].astype(BF16), sc_conv_w[0].astype(F32),
                     sc_w_out[0].astype(BF16), row(mlp_norm[1]), mlp_w_up[1].astype(BF16),
                     mlp_w_down[1].astype(BF16), row(final_norm))
```

```python
import functools
import math

import jax
import jax.numpy as jnp
from jax import lax
from jax.experimental import pallas as pl
from jax.experimental.pallas import tpu as pltpu

D_MODEL = 1024
CONV_A_CH = 512
CONV_A_WIDTH = 31
LN_EPS = 1e-5
HEAD_DIM = 64
HEADS_PER_GROUP = 8
DILATED_GROUPS = ((128, 1), (512, 4), (2048, 16))
N_GROUPS = 3
ATTN_OUT = HEADS_PER_GROUP * HEAD_DIM
NUM_BUCKETS = 32
REL_MAX_DISTANCE = 2048
SC_CONV_WIDTH = 3
D_FF = 4 * D_MODEL
RMS_EPS = 1e-6
NEG_INF = -1e30

STEPS = 128
LANES = 128
SUBLANES = 8
VMEM_LIMIT = 56 * 1024 * 1024

TM = 512
TJ = 512
CONV_HALO = 32
CONV_ROWS = 64
FF_CHUNK = 1024
SLABS = ATTN_OUT // LANES

BF16 = jnp.bfloat16
F32 = jnp.float32


def _dot(a, b):
    return jnp.dot(a, b, preferred_element_type=F32)


def _dot_nt(a, b):
    return lax.dot_general(a, b, (((1,), (1,)), ((), ())), preferred_element_type=F32)


def _rms_norm(x, g):
    y = x * lax.rsqrt(jnp.mean(x * x, axis=-1, keepdims=True) + RMS_EPS)
    return y * g


def _mlp(h, g_ref, wup_ref, wdown_ref, z_scr):
    xn = _rms_norm(h, g_ref[...]).astype(BF16)
    for c in range(D_FF // FF_CHUNK):
        cols = slice(c * FF_CHUNK, (c + 1) * FF_CHUNK)
        z = jnp.maximum(_dot(xn, wup_ref[:, cols]), 0.0)
        z_scr[:, cols] = (z * z).astype(BF16)
    return h + _dot(z_scr[...], wdown_ref[...])


def _conv_chunk(r0, hbuf, cw_ref, cb_ref, lng_ref, lnb_ref, ya_ref):
    lead = CONV_HALO - (CONV_A_WIDTH - 1)
    accs = []
    for s in range(CONV_A_CH // LANES):
        cols = slice(s * LANES, (s + 1) * LANES)
        acc = jnp.broadcast_to(cb_ref[:, cols], (CONV_ROWS, LANES))
        for k in range(CONV_A_WIDTH):
            acc = acc + cw_ref[k:k + 1, cols] * hbuf[s, r0 + lead + k:r0 + lead + k + CONV_ROWS, :]
        accs.append(acc)
    acc = jnp.concatenate(accs, axis=1)
    mu = jnp.mean(acc, axis=-1, keepdims=True)
    cen = acc - mu
    var = jnp.mean(cen * cen, axis=-1, keepdims=True)
    y = cen * lax.rsqrt(var + LN_EPS) * lng_ref[...] + lnb_ref[...]
    ya_ref[0, r0:r0 + CONV_ROWS, :] = (y * jax.nn.sigmoid(y)).astype(BF16)


def _residue_major_inputs(xf, xn_perm, xslab):
    n_slabs = D_MODEL // LANES
    xn_perm[0] = xf.astype(BF16)
    for s in range(n_slabs):
        xslab[0, s] = xf[:, s * LANES:(s + 1) * LANES]
    residues = [[0]]
    for g in range(1, N_GROUPS):
        prev_dil, dil = DILATED_GROUPS[g - 1][1], DILATED_GROUPS[g][1]
        ratio, rows = dil // prev_dil, TM // dil
        src, dst = (g - 1) % 2, g % 2
        res = []
        for blk, blk_res in enumerate(residues[-1]):
            for r in range(ratio):
                b = blk * ratio + r
                res.append(blk_res + prev_dil * r)
                for s in range(n_slabs):
                    piece = xslab[src, s, pl.ds(blk * (TM // prev_dil) + r, rows, stride=ratio), :]
                    xn_perm[g, b * rows:(b + 1) * rows, s * LANES:(s + 1) * LANES] = piece.astype(BF16)
                    if g + 1 < N_GROUPS:
                        xslab[dst, s, b * rows:(b + 1) * rows, :] = piece
        residues.append(res)
    return residues


def _qkv_chunk(xn_perm, w_ref, which, g, residues, out_ref):
    lo = 2 * CONV_A_CH + (which * N_GROUPS + g) * ATTN_OUT
    acc = _dot(xn_perm[g], w_ref[:, lo:lo + ATTN_OUT]).astype(BF16)
    rows = TM // len(residues)
    for b, r in enumerate(residues):
        out_ref[0, r, :, which * ATTN_OUT:(which + 1) * ATTN_OUT] = acc[b * rows:(b + 1) * rows]


def _inproj_ab_kernel(x_ref, g_ref, w_ref, cw_ref, cb_ref, lng_ref, lnb_ref,
                      ya_ref, qkv1_ref, qkv2_ref, qkv3_ref, hbuf, xn_perm, xslab):
    @pl.when(pl.program_id(1) == 0)
    def _():
        hbuf[:, 0:CONV_HALO, :] = jnp.zeros((CONV_A_CH // LANES, CONV_HALO, LANES), F32)

    xf = _rms_norm(x_ref[0], g_ref[...])
    xn = xf.astype(BF16)
    val = _dot(xn, w_ref[:, :CONV_A_CH])
    gate = _dot(xn, w_ref[:, CONV_A_CH:2 * CONV_A_CH])
    glu = val * jax.nn.sigmoid(gate)
    for s in range(CONV_A_CH // LANES):
        hbuf[s, CONV_HALO:CONV_HALO + TM, :] = glu[:, s * LANES:(s + 1) * LANES]
    residues = _residue_major_inputs(xf, xn_perm, xslab)

    for r0 in range(0, TM, CONV_ROWS):
        _conv_chunk(r0, hbuf, cw_ref, cb_ref, lng_ref, lnb_ref, ya_ref)
    hbuf[:, 0:CONV_HALO, :] = hbuf[:, TM:TM + CONV_HALO, :]

    outs = (qkv1_ref, qkv2_ref, qkv3_ref)
    for which in range(3):
        for g in range(N_GROUPS):
            _qkv_chunk(xn_perm, w_ref, which, g, residues[g], outs[g])


def _inproj_ab(h, g, w, cw, cb, lng, lnb):
    bsz, seq, d = h.shape

    def whole(a):
        return pl.BlockSpec(a.shape, lambda b, s: (0,) * a.ndim)

    for (_, prev_dil), (_, dil) in zip(DILATED_GROUPS, DILATED_GROUPS[1:]):
        assert dil % prev_dil == 0 and TM % (2 * SUBLANES * dil) == 0
    assert DILATED_GROUPS[0][1] == 1
    return pl.pallas_call(
        _inproj_ab_kernel,
        grid=(bsz, seq // TM),
        in_specs=[pl.BlockSpec((1, TM, d), lambda b, s: (b, s, 0))]
        + [whole(a) for a in (g, w, cw, cb, lng, lnb)],
        out_specs=[pl.BlockSpec((1, TM, CONV_A_CH), lambda b, s: (b, s, 0))]
        + [pl.BlockSpec((1, dil, TM // dil, 3 * ATTN_OUT), lambda b, s: (b, 0, s, 0))
           for _, dil in DILATED_GROUPS],
        out_shape=[jax.ShapeDtypeStruct((bsz, seq, CONV_A_CH), BF16)]
        + [jax.ShapeDtypeStruct((bsz, dil, seq // dil, 3 * ATTN_OUT), BF16)
           for _, dil in DILATED_GROUPS],
        scratch_shapes=[
            pltpu.VMEM((CONV_A_CH // LANES, CONV_HALO + TM, LANES), F32),
            pltpu.VMEM((N_GROUPS, TM, D_MODEL), BF16),
            pltpu.VMEM((2, D_MODEL // LANES, TM, LANES), F32),
        ],
        compiler_params=pltpu.CompilerParams(
            dimension_semantics=("arbitrary", "arbitrary"),
            vmem_limit_bytes=VMEM_LIMIT),
        name="inproj_ab",
    )(h, g, w, cw, cb, lng, lnb)


def _attn_kernel(bucket_ref, rel_ref, q_ref, k_ref, kp_ref, v_ref, vp_ref, o_ref, lse_ref,
                 bias_scr, *, head0):
    pair = 2 * STEPS

    @pl.when((pl.program_id(0) == 0) & (pl.program_id(1) == 0) & (pl.program_id(2) == 0))
    def _():
        bucket = bucket_ref[...]
        for h in range(HEADS_PER_GROUP):
            acc = jnp.full((STEPS, pair), NEG_INF, F32)
            for b in range(NUM_BUCKETS):
                acc = jnp.where(bucket == b, rel_ref[b, head0 + h], acc)
            bias_scr[h * STEPS:(h + 1) * STEPS, :] = acc

    first_tile = pl.program_id(2) == 0
    lane = lax.broadcasted_iota(jnp.int32, (STEPS, LANES), 1)
    keep_lo = jnp.where(lax.broadcasted_iota(jnp.int32, (1, LANES), 1) < HEAD_DIM, 1.0, 0.0).astype(BF16)
    keep_hi = jnp.where(lax.broadcasted_iota(jnp.int32, (1, LANES), 1) < HEAD_DIM, 0.0, 1.0).astype(BF16)
    ones = jnp.ones((pair, LANES), BF16)
    for i in range(TJ // STEPS):
        rows = slice(i * STEPS, (i + 1) * STEPS)
        prev = slice((i - 1) * STEPS, i * STEPS)
        scores = []
        for j in range(HEADS_PER_GROUP // 2):
            cols = slice(j * LANES, (j + 1) * LANES)
            qp = q_ref[0, 0, rows, cols]
            q2 = jnp.concatenate([qp * keep_lo, qp * keep_hi], axis=0)
            kprev = kp_ref[0, 0, :, cols] if i == 0 else k_ref[0, 0, prev, cols]
            k2 = jnp.concatenate([kprev, k_ref[0, 0, rows, cols]], axis=0)
            scores.append(_dot_nt(q2, k2) + bias_scr[j * pair:(j + 1) * pair, :])
        probs, maxes = [], []
        for j in range(HEADS_PER_GROUP // 2):
            sp, sc = scores[j][:, :STEPS], scores[j][:, STEPS:]
            if i == 0:
                sp = jnp.where(first_tile, NEG_INF, sp)
            mx = jnp.max(jnp.maximum(sp, sc), axis=-1, keepdims=True)
            probs.append(jnp.concatenate([jnp.exp(sp - mx), jnp.exp(sc - mx)], axis=1).astype(BF16))
            maxes.append(mx)
        tile_m = jnp.zeros((STEPS, LANES), F32)
        tile_den = jnp.ones((STEPS, LANES), F32)
        for j in range(HEADS_PER_GROUP // 2):
            cols = slice(j * LANES, (j + 1) * LANES)
            vprev = vp_ref[0, 0, :, cols] if i == 0 else v_ref[0, 0, prev, cols]
            v2 = jnp.concatenate([vprev, v_ref[0, 0, rows, cols]], axis=0)
            o = _dot(probs[j], jnp.concatenate([v2, ones], axis=1))
            den = o[:, LANES:]
            on = o[:, :LANES] / den
            o_ref[0, 0, rows, cols] = jnp.where(lane < HEAD_DIM, on[:STEPS], on[STEPS:]).astype(BF16)
            for e in range(2):
                part = slice(e * STEPS, (e + 1) * STEPS)
                tile_m = jnp.where(lane == 2 * j + e, maxes[j][part], tile_m)
                tile_den = jnp.where(lane == 2 * j + e, den[part], tile_den)
        lse_ref[0, 0, rows, :] = tile_m + jnp.log(tile_den)


def _attention_group(qkv, bucket, rel_bias, g):
    bsz, dilation, sd, _ = qkv.shape
    per_tile = TJ // STEPS

    def cur(col):
        return pl.BlockSpec((1, 1, TJ, ATTN_OUT), lambda b, r, t: (b, r, t, col))

    def prev(col):
        return pl.BlockSpec((1, 1, STEPS, ATTN_OUT),
                            lambda b, r, t: (b, r, jnp.maximum(t * per_tile - 1, 0), col))

    return pl.pallas_call(
        functools.partial(_attn_kernel, head0=g * HEADS_PER_GROUP),
        grid=(bsz, dilation, sd // TJ),
        in_specs=[
            pl.BlockSpec(bucket.shape, lambda b, r, t: (0, 0)),
            pl.BlockSpec(memory_space=pltpu.SMEM),
            cur(0), cur(1), prev(1), cur(2), prev(2),
        ],
        out_specs=[
            pl.BlockSpec((1, 1, TJ, ATTN_OUT), lambda b, r, t: (b, r, t, 0)),
            pl.BlockSpec((1, 1, TJ, LANES), lambda b, r, t: (b, r, t, 0)),
        ],
        out_shape=[
            jax.ShapeDtypeStruct((bsz, dilation, sd, ATTN_OUT), BF16),
            jax.ShapeDtypeStruct((bsz, dilation, sd, LANES), F32),
        ],
        scratch_shapes=[pltpu.VMEM((HEADS_PER_GROUP * STEPS, 2 * STEPS), F32)],
        compiler_params=pltpu.CompilerParams(
            dimension_semantics=("arbitrary", "arbitrary", "arbitrary"),
            vmem_limit_bytes=VMEM_LIMIT),
        name=f"attn_d{dilation}",
    )(bucket, rel_bias, qkv, qkv, qkv, qkv, qkv)


def _t5_causal_bucket(n):
    max_exact = NUM_BUCKETS // 2
    nf = jnp.maximum(n, 1).astype(F32)
    large = max_exact + (jnp.log(nf / max_exact) / math.log(REL_MAX_DISTANCE / max_exact)
                         * (NUM_BUCKETS - max_exact)).astype(jnp.int32)
    return jnp.where(n < max_exact, n, jnp.minimum(large, NUM_BUCKETS - 1))


def _band_buckets(dilation):
    a_idx = jnp.arange(STEPS)[:, None]
    c_idx = jnp.arange(2 * STEPS)[None, :]
    m = a_idx + STEPS - c_idx
    band = (m >= 0) & (m <= STEPS)
    return jnp.where(band, _t5_causal_bucket(jnp.clip(m, 0, STEPS) * dilation), -1)


def _ab_tail_kernel(ya_ref, o1_ref, o2_ref, o3_ref, l1_ref, l2_ref, l3_ref, h_ref,
                    wout_ref, mg_ref, wup_ref, wdown_ref, out_ref, onat, lnat, z_scr):
    strided = [(o_ref, l_ref, dil) for (o_ref, l_ref, (_, dil))
               in zip((o1_ref, o2_ref, o3_ref), (l1_ref, l2_ref, l3_ref), DILATED_GROUPS) if dil > 1]
    for n, (o_ref, l_ref, dil) in enumerate(strided):
        rows = TM // dil
        for r in range(dil):
            lnat[n, pl.ds(r, rows, stride=dil), :] = l_ref[0, r]
            for s in range(SLABS):
                onat[n, s, pl.ds(r, rows, stride=dil), :] = (
                    o_ref[0, r, :, s * LANES:(s + 1) * LANES].astype(F32))

    outs, lses, n = [], [], 0
    for o_ref, l_ref, (_, dil) in zip((o1_ref, o2_ref, o3_ref), (l1_ref, l2_ref, l3_ref),
                                      DILATED_GROUPS):
        if dil == 1:
            outs.append(o_ref[0, 0].astype(F32))
            lses.append(l_ref[0, 0])
        else:
            outs.append(jnp.concatenate([onat[n, s] for s in range(SLABS)], axis=1))
            lses.append(lnat[n])
            n += 1

    mx = jnp.maximum(jnp.maximum(lses[0], lses[1]), lses[2])
    es = [jnp.exp(l - mx) for l in lses]
    inv = 1.0 / (es[0] + es[1] + es[2])
    head_of_col = lax.broadcasted_iota(jnp.int32, (LANES, ATTN_OUT), 1) // HEAD_DIM
    row = lax.broadcasted_iota(jnp.int32, (LANES, ATTN_OUT), 0)
    spread = jnp.where(head_of_col == row, 1.0, 0.0).astype(BF16)

    def expand(w):
        hi = w.astype(BF16)
        lo = (w - hi.astype(F32)).astype(BF16)
        return _dot(hi, spread) + _dot(lo, spread)

    yb = (expand(es[0] * inv) * outs[0] + expand(es[1] * inv) * outs[1]
          + expand(es[2] * inv) * outs[2])
    y = _dot(ya_ref[0], wout_ref[:CONV_A_CH, :]) + _dot(yb.astype(BF16), wout_ref[CONV_A_CH:, :])
    out_ref[0] = _mlp(h_ref[0] + y, mg_ref, wup_ref, wdown_ref, z_scr)


def _ab_tail(ya, outs, lses, h, wout, mg, wup, wdown):
    bsz, seq, d = h.shape

    def tile(width):
        return pl.BlockSpec((1, TM, width), lambda b, s: (b, s, 0))

    def residue_tile(dil, width):
        return pl.BlockSpec((1, dil, TM // dil, width), lambda b, s: (b, 0, s, 0))

    def whole(a):
        return pl.BlockSpec(a.shape, lambda b, s: (0,) * a.ndim)

    n_strided = sum(1 for _, dil in DILATED_GROUPS if dil > 1)
    return pl.pallas_call(
        _ab_tail_kernel,
        grid=(bsz, seq // TM),
        in_specs=[tile(CONV_A_CH)]
        + [residue_tile(dil, ATTN_OUT) for _, dil in DILATED_GROUPS]
        + [residue_tile(dil, LANES) for _, dil in DILATED_GROUPS]
        + [tile(d)] + [whole(a) for a in (wout, mg, wup, wdown)],
        out_specs=tile(d),
        out_shape=jax.ShapeDtypeStruct((bsz, seq, d), F32),
        scratch_shapes=[
            pltpu.VMEM((n_strided, SLABS, TM, LANES), F32),
            pltpu.VMEM((n_strided, TM, LANES), F32),
            pltpu.VMEM((TM, D_FF), BF16),
        ],
        compiler_params=pltpu.CompilerParams(
            dimension_semantics=("arbitrary", "arbitrary"),
            vmem_limit_bytes=VMEM_LIMIT),
        name="ab_tail",
    )(ya, *outs, *lses, h, wout, mg, wup, wdown)


def _sc_layer_kernel(h_ref, ng_ref, win_ref, cw_ref, wout_ref, mg_ref, wup_ref, wdown_ref,
                     fg_ref, out_ref, cvbuf, z_scr):
    halo = SUBLANES

    @pl.when(pl.program_id(1) == 0)
    def _():
        cvbuf[0:halo, :] = jnp.zeros((halo, D_MODEL), F32)

    h = h_ref[0]
    xn = _rms_norm(h, ng_ref[...]).astype(BF16)
    c_gate = _dot(xn, win_ref[:, D_MODEL:2 * D_MODEL])
    val = _dot(xn, win_ref[:, 2 * D_MODEL:])
    cvbuf[halo:halo + TM, :] = c_gate * val
    first = halo - (SC_CONV_WIDTH - 1)
    conv = cw_ref[0:1, :] * cvbuf[first:first + TM, :]
    for k in range(1, SC_CONV_WIDTH):
        conv = conv + cw_ref[k:k + 1, :] * cvbuf[first + k:first + k + TM, :]
    cvbuf[0:halo, :] = cvbuf[TM:TM + halo, :]
    b_gate = _dot(xn, win_ref[:, :D_MODEL])
    h1 = h + _dot((b_gate * conv).astype(BF16), wout_ref[...])
    h2 = _mlp(h1, mg_ref, wup_ref, wdown_ref, z_scr)
    out_ref[0] = _rms_norm(h2, fg_ref[...])


def _sc_layer(h, ng, win, cw, wout, mg, wup, wdown, fg):
    bsz, seq, d = h.shape

    def whole(a):
        return pl.BlockSpec(a.shape, lambda b, s: (0,) * a.ndim)

    tile = pl.BlockSpec((1, TM, d), lambda b, s: (b, s, 0))
    return pl.pallas_call(
        _sc_layer_kernel,
        grid=(bsz, seq // TM),
        in_specs=[tile] + [whole(a) for a in (ng, win, cw, wout, mg, wup, wdown, fg)],
        out_specs=tile,
        out_shape=jax.ShapeDtypeStruct((bsz, seq, d), F32),
        scratch_shapes=[
            pltpu.VMEM((SUBLANES + TM, d), F32),
            pltpu.VMEM((TM, D_FF), BF16),
        ],
        compiler_params=pltpu.CompilerParams(
            dimension_semantics=("arbitrary", "arbitrary"),
            vmem_limit_bytes=VMEM_LIMIT),
        name="sc_layer",
    )(h, ng, win, cw, wout, mg, wup, wdown, fg)


def kernel(x, rel_bias, ab_norm, ab_w_in, ab_conv_w, ab_conv_b, ab_ln_g, ab_ln_b, ab_w_out,
           sc_norm, sc_w_in, sc_conv_w, sc_w_out, mlp_norm, mlp_w_up, mlp_w_down, final_norm):
    assert x.shape[1] % (TJ * DILATED_GROUPS[-1][1]) == 0 and x.shape[2] == D_MODEL
    assert ab_norm.shape[0] == 1 and sc_norm.shape[0] == 1 and mlp_norm.shape[0] == 2

    def row(v):
        return v.reshape(1, -1).astype(F32)

    q_cols = (jnp.arange(ab_w_in.shape[-1]) >= 2 * CONV_A_CH) & (
        jnp.arange(ab_w_in.shape[-1]) < 2 * CONV_A_CH + N_GROUPS * ATTN_OUT)
    w_in = (ab_w_in[0] * jnp.where(q_cols, HEAD_DIM ** -0.5, 1.0)).astype(BF16)
    ya, *qkvs = _inproj_ab(x, row(ab_norm[0]), w_in, ab_conv_w[0].astype(F32),
                           row(ab_conv_b[0]), row(ab_ln_g[0]), row(ab_ln_b[0]))
    outs, lses = [], []
    for g, (_, dilation) in enumerate(DILATED_GROUPS):
        o, lse = _attention_group(qkvs[g], _band_buckets(dilation), rel_bias.astype(F32), g)
        outs.append(o)
        lses.append(lse)
    h = _ab_tail(ya, outs, lses, x, ab_w_out[0].astype(BF16), row(mlp_norm[0]),
                 mlp_w_up[0].astype(BF16), mlp_w_down[0].astype(BF16))
    return _sc_layer(h, row(sc_norm[0]), sc_w_in[0].astype(BF16), sc_conv_w[0].astype(F32),
                     sc_w_out[0].astype(BF16), row(mlp_norm[1]), mlp_w_up[1].astype(BF16),
                     mlp_w_down[1].astype(BF16), row(final_norm))
```

```python
import functools
import math

import jax
import jax.numpy as jnp
from jax import lax
from jax.experimental import pallas as pl
from jax.experimental.pallas import tpu as pltpu

D_MODEL = 1024
CONV_A_CH = 512
CONV_A_WIDTH = 31
LN_EPS = 1e-5
HEAD_DIM = 64
HEADS_PER_GROUP = 8
DILATED_GROUPS = ((128, 1), (512, 4), (2048, 16))
N_GROUPS = 3
ATTN_OUT = HEADS_PER_GROUP * HEAD_DIM
NUM_BUCKETS = 32
REL_MAX_DISTANCE = 2048
SC_CONV_WIDTH = 3
D_FF = 4 * D_MODEL
RMS_EPS = 1e-6
NEG_INF = -1e30

STEPS = 128
LANES = 128
SUBLANES = 8
VMEM_LIMIT = 56 * 1024 * 1024

TM = 512
ATTN_STEP_ROWS = 1024
CONV_HALO = 32
CONV_ROWS = 64
FF_CHUNK = 1024
SLABS = ATTN_OUT // LANES

BF16 = jnp.bfloat16
F32 = jnp.float32


def _dot(a, b):
    return jnp.dot(a, b, preferred_element_type=F32)


def _dot_nt(a, b):
    return lax.dot_general(a, b, (((1,), (1,)), ((), ())), preferred_element_type=F32)


def _rms_norm(x, g):
    y = x * lax.rsqrt(jnp.mean(x * x, axis=-1, keepdims=True) + RMS_EPS)
    return y * g


def _mlp(h, g_ref, wup_ref, wdown_ref, z_scr):
    xn = _rms_norm(h, g_ref[...]).astype(BF16)
    for c in range(D_FF // FF_CHUNK):
        cols = slice(c * FF_CHUNK, (c + 1) * FF_CHUNK)
        z = jnp.maximum(_dot(xn, wup_ref[:, cols]), 0.0)
        z_scr[:, cols] = (z * z).astype(BF16)
    return h + _dot(z_scr[...], wdown_ref[...])


def _conv_chunk(r0, hbuf, cw_ref, cb_ref, lng_ref, lnb_ref, ya_ref):
    lead = CONV_HALO - (CONV_A_WIDTH - 1)
    nblk = CONV_ROWS // SUBLANES
    accs = []
    for s in range(CONV_A_CH // LANES):
        cols = slice(s * LANES, (s + 1) * LANES)
        taps = [jnp.broadcast_to(cw_ref[k:k + 1, cols], (SUBLANES, LANES)) for k in range(CONV_A_WIDTH)]
        blk = [jnp.broadcast_to(cb_ref[:, cols], (SUBLANES, LANES)) for _ in range(nblk)]
        for o in range(SUBLANES * (nblk - 1) + CONV_A_WIDTH):
            win = hbuf[s, r0 + lead + o:r0 + lead + o + SUBLANES, :]
            for i in range(nblk):
                k = o - SUBLANES * i
                if 0 <= k < CONV_A_WIDTH:
                    blk[i] = blk[i] + taps[k] * win
        accs.append(jnp.concatenate(blk, axis=0))
    acc = jnp.concatenate(accs, axis=1)
    mu = jnp.mean(acc, axis=-1, keepdims=True)
    cen = acc - mu
    var = jnp.mean(cen * cen, axis=-1, keepdims=True)
    y = cen * lax.rsqrt(var + LN_EPS) * lng_ref[...] + lnb_ref[...]
    ya_ref[0, r0:r0 + CONV_ROWS, :] = (y * jax.nn.sigmoid(y)).astype(BF16)


def _residue_major_inputs(xf, xn_perm, xslab):
    n_slabs = D_MODEL // LANES
    xn_perm[0] = xf.astype(BF16)
    for s in range(n_slabs):
        xslab[0, s] = xf[:, s * LANES:(s + 1) * LANES]
    residues = [[0]]
    for g in range(1, N_GROUPS):
        prev_dil, dil = DILATED_GROUPS[g - 1][1], DILATED_GROUPS[g][1]
        ratio, rows = dil // prev_dil, TM // dil
        src, dst = (g - 1) % 2, g % 2
        res = []
        for blk, blk_res in enumerate(residues[-1]):
            for r in range(ratio):
                b = blk * ratio + r
                res.append(blk_res + prev_dil * r)
                for s in range(n_slabs):
                    piece = xslab[src, s, pl.ds(blk * (TM // prev_dil) + r, rows, stride=ratio), :]
                    xn_perm[g, b * rows:(b + 1) * rows, s * LANES:(s + 1) * LANES] = piece.astype(BF16)
                    if g + 1 < N_GROUPS:
                        xslab[dst, s, b * rows:(b + 1) * rows, :] = piece
        residues.append(res)
    return residues


def _qkv_chunk(xn_perm, w_ref, which, g, residues, out_ref):
    lo = 2 * CONV_A_CH + (which * N_GROUPS + g) * ATTN_OUT
    acc = _dot(xn_perm[g], w_ref[:, lo:lo + ATTN_OUT]).astype(BF16)
    rows = TM // len(residues)
    for b, r in enumerate(residues):
        out_ref[0, r, :, which * ATTN_OUT:(which + 1) * ATTN_OUT] = acc[b * rows:(b + 1) * rows]


def _inproj_ab_kernel(x_ref, g_ref, w_ref, cw_ref, cb_ref, lng_ref, lnb_ref,
                      ya_ref, qkv1_ref, qkv2_ref, qkv3_ref, hbuf, xn_perm, xslab):
    @pl.when(pl.program_id(1) == 0)
    def _():
        hbuf[:, 0:CONV_HALO, :] = jnp.zeros((CONV_A_CH // LANES, CONV_HALO, LANES), F32)

    xf = _rms_norm(x_ref[0], g_ref[...])
    xn = xf.astype(BF16)
    val = _dot(xn, w_ref[:, :CONV_A_CH])
    gate = _dot(xn, w_ref[:, CONV_A_CH:2 * CONV_A_CH])
    glu = val * jax.nn.sigmoid(gate)
    for s in range(CONV_A_CH // LANES):
        hbuf[s, CONV_HALO:CONV_HALO + TM, :] = glu[:, s * LANES:(s + 1) * LANES]
    residues = _residue_major_inputs(xf, xn_perm, xslab)

    for r0 in range(0, TM, CONV_ROWS):
        _conv_chunk(r0, hbuf, cw_ref, cb_ref, lng_ref, lnb_ref, ya_ref)
    hbuf[:, 0:CONV_HALO, :] = hbuf[:, TM:TM + CONV_HALO, :]

    outs = (qkv1_ref, qkv2_ref, qkv3_ref)
    for which in range(3):
        for g in range(N_GROUPS):
            _qkv_chunk(xn_perm, w_ref, which, g, residues[g], outs[g])


def _inproj_ab(h, g, w, cw, cb, lng, lnb):
    bsz, seq, d = h.shape

    def whole(a):
        return pl.BlockSpec(a.shape, lambda b, s: (0,) * a.ndim)

    for (_, prev_dil), (_, dil) in zip(DILATED_GROUPS, DILATED_GROUPS[1:]):
        assert dil % prev_dil == 0 and TM % (2 * SUBLANES * dil) == 0
    assert DILATED_GROUPS[0][1] == 1
    return pl.pallas_call(
        _inproj_ab_kernel,
        grid=(bsz, seq // TM),
        in_specs=[pl.BlockSpec((1, TM, d), lambda b, s: (b, s, 0))]
        + [whole(a) for a in (g, w, cw, cb, lng, lnb)],
        out_specs=[pl.BlockSpec((1, TM, CONV_A_CH), lambda b, s: (b, s, 0))]
        + [pl.BlockSpec((1, dil, TM // dil, 3 * ATTN_OUT), lambda b, s: (b, 0, s, 0))
           for _, dil in DILATED_GROUPS],
        out_shape=[jax.ShapeDtypeStruct((bsz, seq, CONV_A_CH), BF16)]
        + [jax.ShapeDtypeStruct((bsz, dil, seq // dil, 3 * ATTN_OUT), BF16)
           for _, dil in DILATED_GROUPS],
        scratch_shapes=[
            pltpu.VMEM((CONV_A_CH // LANES, CONV_HALO + TM, LANES), F32),
            pltpu.VMEM((N_GROUPS, TM, D_MODEL), BF16),
            pltpu.VMEM((2, D_MODEL // LANES, TM, LANES), F32),
        ],
        compiler_params=pltpu.CompilerParams(
            dimension_semantics=("arbitrary", "arbitrary"),
            vmem_limit_bytes=VMEM_LIMIT),
        name="inproj_ab",
    )(h, g, w, cw, cb, lng, lnb)


def _attn_kernel(bucket_ref, rel_ref, q_ref, k_ref, kp_ref, v_ref, vp_ref, o_ref, lse_ref,
                 bias_scr, *, head0):
    pair = 2 * STEPS

    @pl.when((pl.program_id(0) == 0) & (pl.program_id(1) == 0) & (pl.program_id(2) == 0))
    def _():
        bucket = bucket_ref[...]
        for h in range(HEADS_PER_GROUP):
            acc = jnp.full((STEPS, pair), NEG_INF, F32)
            for b in range(NUM_BUCKETS):
                acc = jnp.where(bucket == b, rel_ref[b, head0 + h], acc)
            bias_scr[h * STEPS:(h + 1) * STEPS, :] = acc

    first_tile = pl.program_id(2) == 0
    lane = lax.broadcasted_iota(jnp.int32, (STEPS, LANES), 1)
    keep_lo = jnp.where(lax.broadcasted_iota(jnp.int32, (1, LANES), 1) < HEAD_DIM, 1.0, 0.0).astype(BF16)
    keep_hi = jnp.where(lax.broadcasted_iota(jnp.int32, (1, LANES), 1) < HEAD_DIM, 0.0, 1.0).astype(BF16)
    ones = jnp.ones((pair, LANES), BF16)
    _, n_res, n_rows, _ = q_ref.shape
    for rr, i in [(rr, i) for rr in range(n_res) for i in range(n_rows // STEPS)]:
        rows = slice(i * STEPS, (i + 1) * STEPS)
        prev = slice((i - 1) * STEPS, i * STEPS)
        scores = []
        for j in range(HEADS_PER_GROUP // 2):
            cols = slice(j * LANES, (j + 1) * LANES)
            qp = q_ref[0, rr, rows, cols]
            q2 = jnp.concatenate([qp * keep_lo, qp * keep_hi], axis=0)
            kprev = kp_ref[0, rr, :, cols] if i == 0 else k_ref[0, rr, prev, cols]
            k2 = jnp.concatenate([kprev, k_ref[0, rr, rows, cols]], axis=0)
            scores.append(_dot_nt(q2, k2) + bias_scr[j * pair:(j + 1) * pair, :])
        probs, maxes = [], []
        for j in range(HEADS_PER_GROUP // 2):
            sp, sc = scores[j][:, :STEPS], scores[j][:, STEPS:]
            if i == 0:
                sp = jnp.where(first_tile, NEG_INF, sp)
            mx = jnp.max(jnp.maximum(sp, sc), axis=-1, keepdims=True)
            probs.append(jnp.concatenate([jnp.exp(sp - mx), jnp.exp(sc - mx)], axis=1).astype(BF16))
            maxes.append(mx)
        tile_m = jnp.zeros((STEPS, LANES), F32)
        tile_den = jnp.ones((STEPS, LANES), F32)
        for j in range(HEADS_PER_GROUP // 2):
            cols = slice(j * LANES, (j + 1) * LANES)
            vprev = vp_ref[0, rr, :, cols] if i == 0 else v_ref[0, rr, prev, cols]
            v2 = jnp.concatenate([vprev, v_ref[0, rr, rows, cols]], axis=0)
            o = _dot(probs[j], jnp.concatenate([v2, ones], axis=1))
            den = o[:, LANES:]
            on = o[:, :LANES] / den
            o_ref[0, rr, rows, cols] = jnp.where(lane < HEAD_DIM, on[:STEPS], on[STEPS:]).astype(BF16)
            for e in range(2):
                part = slice(e * STEPS, (e + 1) * STEPS)
                tile_m = jnp.where(lane == 2 * j + e, maxes[j][part], tile_m)
                tile_den = jnp.where(lane == 2 * j + e, den[part], tile_den)
        lse_ref[0, rr, rows, :] = tile_m + jnp.log(tile_den)


def _attention_group(qkv, bucket, rel_bias, g):
    bsz, dilation, sd, _ = qkv.shape
    n_rows = min(sd, ATTN_STEP_ROWS)
    n_res = ATTN_STEP_ROWS // n_rows
    assert sd % n_rows == 0 and dilation % n_res == 0 and n_rows % STEPS == 0
    per_tile = n_rows // STEPS

    def cur(col):
        return pl.BlockSpec((1, n_res, n_rows, ATTN_OUT), lambda b, r, t: (b, r, t, col))

    def prev(col):
        return pl.BlockSpec((1, n_res, STEPS, ATTN_OUT),
                            lambda b, r, t: (b, r, jnp.maximum(t * per_tile - 1, 0), col))

    return pl.pallas_call(
        functools.partial(_attn_kernel, head0=g * HEADS_PER_GROUP),
        grid=(bsz, dilation // n_res, sd // n_rows),
        in_specs=[
            pl.BlockSpec(bucket.shape, lambda b, r, t: (0, 0)),
            pl.BlockSpec(memory_space=pltpu.SMEM),
            cur(0), cur(1), prev(1), cur(2), prev(2),
        ],
        out_specs=[
            pl.BlockSpec((1, n_res, n_rows, ATTN_OUT), lambda b, r, t: (b, r, t, 0)),
            pl.BlockSpec((1, n_res, n_rows, LANES), lambda b, r, t: (b, r, t, 0)),
        ],
        out_shape=[
            jax.ShapeDtypeStruct((bsz, dilation, sd, ATTN_OUT), BF16),
            jax.ShapeDtypeStruct((bsz, dilation, sd, LANES), F32),
        ],
        scratch_shapes=[pltpu.VMEM((HEADS_PER_GROUP * STEPS, 2 * STEPS), F32)],
        compiler_params=pltpu.CompilerParams(
            dimension_semantics=("arbitrary", "arbitrary", "arbitrary"),
            vmem_limit_bytes=VMEM_LIMIT),
        name=f"attn_d{dilation}",
    )(bucket, rel_bias, qkv, qkv, qkv, qkv, qkv)


def _t5_causal_bucket(n):
    max_exact = NUM_BUCKETS // 2
    nf = jnp.maximum(n, 1).astype(F32)
    large = max_exact + (jnp.log(nf / max_exact) / math.log(REL_MAX_DISTANCE / max_exact)
                         * (NUM_BUCKETS - max_exact)).astype(jnp.int32)
    return jnp.where(n < max_exact, n, jnp.minimum(large, NUM_BUCKETS - 1))


def _band_buckets(dilation):
    a_idx = jnp.arange(STEPS)[:, None]
    c_idx = jnp.arange(2 * STEPS)[None, :]
    m = a_idx + STEPS - c_idx
    band = (m >= 0) & (m <= STEPS)
    return jnp.where(band, _t5_causal_bucket(jnp.clip(m, 0, STEPS) * dilation), -1)


def _ab_tail_kernel(ya_ref, o1_ref, o2_ref, o3_ref, l1_ref, l2_ref, l3_ref, h_ref,
                    wout_ref, mg_ref, wup_ref, wdown_ref, out_ref, onat, lnat, z_scr):
    strided = [(o_ref, l_ref, dil) for (o_ref, l_ref, (_, dil))
               in zip((o1_ref, o2_ref, o3_ref), (l1_ref, l2_ref, l3_ref), DILATED_GROUPS) if dil > 1]
    for n, (o_ref, l_ref, dil) in enumerate(strided):
        rows = TM // dil
        for r in range(dil):
            lnat[n, pl.ds(r, rows, stride=dil), :] = l_ref[0, r]
            for s in range(SLABS):
                onat[n, s, pl.ds(r, rows, stride=dil), :] = (
                    o_ref[0, r, :, s * LANES:(s + 1) * LANES].astype(F32))

    outs, lses, n = [], [], 0
    for o_ref, l_ref, (_, dil) in zip((o1_ref, o2_ref, o3_ref), (l1_ref, l2_ref, l3_ref),
                                      DILATED_GROUPS):
        if dil == 1:
            outs.append(o_ref[0, 0].astype(F32))
            lses.append(l_ref[0, 0])
        else:
            outs.append(jnp.concatenate([onat[n, s] for s in range(SLABS)], axis=1))
            lses.append(lnat[n])
            n += 1

    mx = jnp.maximum(jnp.maximum(lses[0], lses[1]), lses[2])
    es = [jnp.exp(l - mx) for l in lses]
    inv = 1.0 / (es[0] + es[1] + es[2])
    head_of_col = lax.broadcasted_iota(jnp.int32, (2 * LANES, ATTN_OUT), 1) // HEAD_DIM
    row = lax.broadcasted_iota(jnp.int32, (2 * LANES, ATTN_OUT), 0) % LANES
    spread = jnp.where(head_of_col == row, 1.0, 0.0).astype(BF16)

    def expand(w):
        hi = w.astype(BF16)
        lo = (w - hi.astype(F32)).astype(BF16)
        return _dot(jnp.concatenate([hi, lo], axis=1), spread)

    yb = outs[-1]
    for e, o in zip(es[:-1], outs[:-1]):
        yb = yb + expand(e * inv) * (o - outs[-1])
    y = _dot(ya_ref[0], wout_ref[:CONV_A_CH, :]) + _dot(yb.astype(BF16), wout_ref[CONV_A_CH:, :])
    out_ref[0] = _mlp(h_ref[0] + y, mg_ref, wup_ref, wdown_ref, z_scr)


def _ab_tail(ya, outs, lses, h, wout, mg, wup, wdown):
    bsz, seq, d = h.shape

    def tile(width):
        return pl.BlockSpec((1, TM, width), lambda b, s: (b, s, 0))

    def residue_tile(dil, width):
        return pl.BlockSpec((1, dil, TM // dil, width), lambda b, s: (b, 0, s, 0))

    def whole(a):
        return pl.BlockSpec(a.shape, lambda b, s: (0,) * a.ndim)

    n_strided = sum(1 for _, dil in DILATED_GROUPS if dil > 1)
    return pl.pallas_call(
        _ab_tail_kernel,
        grid=(bsz, seq // TM),
        in_specs=[tile(CONV_A_CH)]
        + [residue_tile(dil, ATTN_OUT) for _, dil in DILATED_GROUPS]
        + [residue_tile(dil, LANES) for _, dil in DILATED_GROUPS]
        + [tile(d)] + [whole(a) for a in (wout, mg, wup, wdown)],
        out_specs=tile(d),
        out_shape=jax.ShapeDtypeStruct((bsz, seq, d), F32),
        scratch_shapes=[
            pltpu.VMEM((n_strided, SLABS, TM, LANES), F32),
            pltpu.VMEM((n_strided, TM, LANES), F32),
            pltpu.VMEM((TM, D_FF), BF16),
        ],
        compiler_params=pltpu.CompilerParams(
            dimension_semantics=("arbitrary", "arbitrary"),
            vmem_limit_bytes=VMEM_LIMIT),
        name="ab_tail",
    )(ya, *outs, *lses, h, wout, mg, wup, wdown)


def _sc_layer_kernel(h_ref, ng_ref, win_ref, cw_ref, wout_ref, mg_ref, wup_ref, wdown_ref,
                     fg_ref, out_ref, cvbuf, z_scr):
    halo = SUBLANES

    @pl.when(pl.program_id(1) == 0)
    def _():
        cvbuf[0:halo, :] = jnp.zeros((halo, D_MODEL), F32)

    h = h_ref[0]
    xn = _rms_norm(h, ng_ref[...]).astype(BF16)
    c_gate = _dot(xn, win_ref[:, D_MODEL:2 * D_MODEL])
    val = _dot(xn, win_ref[:, 2 * D_MODEL:])
    cvbuf[halo:halo + TM, :] = c_gate * val
    first = halo - (SC_CONV_WIDTH - 1)
    conv = cw_ref[0:1, :] * cvbuf[first:first + TM, :]
    for k in range(1, SC_CONV_WIDTH):
        conv = conv + cw_ref[k:k + 1, :] * cvbuf[first + k:first + k + TM, :]
    cvbuf[0:halo, :] = cvbuf[TM:TM + halo, :]
    b_gate = _dot(xn, win_ref[:, :D_MODEL])
    h1 = h + _dot((b_gate * conv).astype(BF16), wout_ref[...])
    h2 = _mlp(h1, mg_ref, wup_ref, wdown_ref, z_scr)
    out_ref[0] = _rms_norm(h2, fg_ref[...])


def _sc_layer(h, ng, win, cw, wout, mg, wup, wdown, fg):
    bsz, seq, d = h.shape

    def whole(a):
        return pl.BlockSpec(a.shape, lambda b, s: (0,) * a.ndim)

    tile = pl.BlockSpec((1, TM, d), lambda b, s: (b, s, 0))
    return pl.pallas_call(
        _sc_layer_kernel,
        grid=(bsz, seq // TM),
        in_specs=[tile] + [whole(a) for a in (ng, win, cw, wout, mg, wup, wdown, fg)],
        out_specs=tile,
        out_shape=jax.ShapeDtypeStruct((bsz, seq, d), F32),
        scratch_shapes=[
            pltpu.VMEM((SUBLANES + TM, d), F32),
            pltpu.VMEM((TM, D_FF), BF16),
        ],
        compiler_params=pltpu.CompilerParams(
            dimension_semantics=("arbitrary", "arbitrary"),
            vmem_limit_bytes=VMEM_LIMIT),
        name="sc_layer",
    )(h, ng, win, cw, wout, mg, wup, wdown, fg)


def kernel(x, rel_bias, ab_norm, ab_w_in, ab_conv_w, ab_conv_b, ab_ln_g, ab_ln_b, ab_w_out,
           sc_norm, sc_w_in, sc_conv_w, sc_w_out, mlp_norm, mlp_w_up, mlp_w_down, final_norm):
    assert x.shape[1] % (STEPS * DILATED_GROUPS[-1][1]) == 0 and x.shape[2] == D_MODEL
    assert ab_norm.shape[0] == 1 and sc_norm.shape[0] == 1 and mlp_norm.shape[0] == 2

    def row(v):
        return v.reshape(1, -1).astype(F32)

    q_cols = (jnp.arange(ab_w_in.shape[-1]) >= 2 * CONV_A_CH) & (
        jnp.arange(ab_w_in.shape[-1]) < 2 * CONV_A_CH + N_GROUPS * ATTN_OUT)
    w_in = (ab_w_in[0] * jnp.where(q_cols, HEAD_DIM ** -0.5, 1.0)).astype(BF16)
    ya, *qkvs = _inproj_ab(x, row(ab_norm[0]), w_in, ab_conv_w[0].astype(F32),
                           row(ab_conv_b[0]), row(ab_ln_g[0]), row(ab_ln_b[0]))
    outs, lses = [], []
    for g, (_, dilation) in enumerate(DILATED_GROUPS):
        o, lse = _attention_group(qkvs[g], _band_buckets(dilation), rel_bias.astype(F32), g)
        outs.append(o)
        lses.append(lse)
    h = _ab_tail(ya, outs, lses, x, ab_w_out[0].astype(BF16), row(mlp_norm[0]),
                 mlp_w_up[0].astype(BF16), mlp_w_down[0].astype(BF16))
    return _sc_layer(h, row(sc_norm[0]), sc_w_in[0].astype(BF16), sc_conv_w[0].astype(F32),
                     sc_w_out[0].astype(BF16), row(mlp_norm[1]), mlp_w_up[1].astype(BF16),
                     mlp_w_down[1].astype(BF16), row(final_norm))
```

```python
import functools
import math

import jax
import jax.numpy as jnp
from jax import lax
from jax.experimental import pallas as pl
from jax.experimental.pallas import tpu as pltpu

D_MODEL = 1024
CONV_A_CH = 512
CONV_A_WIDTH = 31
LN_EPS = 1e-5
HEAD_DIM = 64
HEADS_PER_GROUP = 8
DILATED_GROUPS = ((128, 1), (512, 4), (2048, 16))
N_GROUPS = 3
ATTN_OUT = HEADS_PER_GROUP * HEAD_DIM
NUM_BUCKETS = 32
REL_MAX_DISTANCE = 2048
SC_CONV_WIDTH = 3
D_FF = 4 * D_MODEL
RMS_EPS = 1e-6
NEG_INF = -1e30

STEPS = 128
LANES = 128
SUBLANES = 8
VMEM_LIMIT = 56 * 1024 * 1024

TM = 512
ATTN_STEP_ROWS = 1024
CONV_HALO = 32
CONV_ROWS = 64
FF_CHUNK = 1024
SLABS = ATTN_OUT // LANES

BF16 = jnp.bfloat16
F32 = jnp.float32


def _dot(a, b):
    return jnp.dot(a, b, preferred_element_type=F32)


def _dot_nt(a, b):
    return lax.dot_general(a, b, (((1,), (1,)), ((), ())), preferred_element_type=F32)


def _rms_norm(x, g):
    y = x * lax.rsqrt(jnp.mean(x * x, axis=-1, keepdims=True) + RMS_EPS)
    return y * g


def _mlp(h, g_ref, wup_ref, wdown_ref, z_scr):
    xn = _rms_norm(h, g_ref[...]).astype(BF16)
    for c in range(D_FF // FF_CHUNK):
        cols = slice(c * FF_CHUNK, (c + 1) * FF_CHUNK)
        z = jnp.maximum(_dot(xn, wup_ref[:, cols]), 0.0)
        z_scr[:, cols] = (z * z).astype(BF16)
    return h + _dot(z_scr[...], wdown_ref[...])


def _conv_chunk(r0, hbuf, cw_ref, cb_ref, lng_ref, lnb_ref, ya_ref):
    lead = CONV_HALO - (CONV_A_WIDTH - 1)
    nblk = CONV_ROWS // SUBLANES
    accs = []
    for s in range(CONV_A_CH // LANES):
        cols = slice(s * LANES, (s + 1) * LANES)
        taps = [jnp.broadcast_to(cw_ref[k:k + 1, cols], (SUBLANES, LANES)) for k in range(CONV_A_WIDTH)]
        blk = [jnp.broadcast_to(cb_ref[:, cols], (SUBLANES, LANES)) for _ in range(nblk)]
        for o in range(SUBLANES * (nblk - 1) + CONV_A_WIDTH):
            win = hbuf[s, r0 + lead + o:r0 + lead + o + SUBLANES, :]
            for i in range(nblk):
                k = o - SUBLANES * i
                if 0 <= k < CONV_A_WIDTH:
                    blk[i] = blk[i] + taps[k] * win
        accs.append(jnp.concatenate(blk, axis=0))
    acc = jnp.concatenate(accs, axis=1)
    mu = jnp.mean(acc, axis=-1, keepdims=True)
    cen = acc - mu
    var = jnp.mean(cen * cen, axis=-1, keepdims=True)
    y = cen * lax.rsqrt(var + LN_EPS) * lng_ref[...] + lnb_ref[...]
    ya_ref[0, r0:r0 + CONV_ROWS, :] = (y * jax.nn.sigmoid(y)).astype(BF16)


def _residue_major_inputs(xf, xn_perm, xslab):
    n_slabs = D_MODEL // LANES
    xn_perm[0] = xf.astype(BF16)
    for s in range(n_slabs):
        xslab[0, s] = xf[:, s * LANES:(s + 1) * LANES]
    residues = [[0]]
    for g in range(1, N_GROUPS):
        prev_dil, dil = DILATED_GROUPS[g - 1][1], DILATED_GROUPS[g][1]
        ratio, rows = dil // prev_dil, TM // dil
        src, dst = (g - 1) % 2, g % 2
        res = []
        for blk, blk_res in enumerate(residues[-1]):
            for r in range(ratio):
                b = blk * ratio + r
                res.append(blk_res + prev_dil * r)
                for s in range(n_slabs):
                    piece = xslab[src, s, pl.ds(blk * (TM // prev_dil) + r, rows, stride=ratio), :]
                    xn_perm[g, b * rows:(b + 1) * rows, s * LANES:(s + 1) * LANES] = piece.astype(BF16)
                    if g + 1 < N_GROUPS:
                        xslab[dst, s, b * rows:(b + 1) * rows, :] = piece
        residues.append(res)
    return residues


def _qkv_chunk(xn_perm, w_ref, which, g, residues, out_ref):
    lo = 2 * CONV_A_CH + (which * N_GROUPS + g) * ATTN_OUT
    acc = _dot(xn_perm[g], w_ref[:, lo:lo + ATTN_OUT]).astype(BF16)
    rows = TM // len(residues)
    for b, r in enumerate(residues):
        out_ref[0, r, :, which * ATTN_OUT:(which + 1) * ATTN_OUT] = acc[b * rows:(b + 1) * rows]


def _inproj_ab_kernel(x_ref, g_ref, w_ref, cw_ref, cb_ref, lng_ref, lnb_ref, *rest):
    n_later = (len(rest) - 7) // 2
    later_f32, rest = rest[:n_later], rest[n_later:]
    ya_ref, qkv1_ref, qkv2_ref, qkv3_ref = rest[:4]
    later_bf16, (hbuf, xn_perm, xslab) = rest[4:4 + n_later], rest[4 + n_later:]
    for src, dst in zip(later_f32, later_bf16):
        dst[...] = src[...].astype(BF16)

    @pl.when(pl.program_id(1) == 0)
    def _():
        hbuf[:, 0:CONV_HALO, :] = jnp.zeros((CONV_A_CH // LANES, CONV_HALO, LANES), F32)

    xf = _rms_norm(x_ref[0], g_ref[...])
    xn = xf.astype(BF16)
    val = _dot(xn, w_ref[:, :CONV_A_CH])
    gate = _dot(xn, w_ref[:, CONV_A_CH:2 * CONV_A_CH])
    glu = val * jax.nn.sigmoid(gate)
    for s in range(CONV_A_CH // LANES):
        hbuf[s, CONV_HALO:CONV_HALO + TM, :] = glu[:, s * LANES:(s + 1) * LANES]
    residues = _residue_major_inputs(xf, xn_perm, xslab)

    for r0 in range(0, TM, CONV_ROWS):
        _conv_chunk(r0, hbuf, cw_ref, cb_ref, lng_ref, lnb_ref, ya_ref)
    hbuf[:, 0:CONV_HALO, :] = hbuf[:, TM:TM + CONV_HALO, :]

    outs = (qkv1_ref, qkv2_ref, qkv3_ref)
    for which in range(3):
        for g in range(N_GROUPS):
            _qkv_chunk(xn_perm, w_ref, which, g, residues[g], outs[g])


def _inproj_ab(h, g, w, cw, cb, lng, lnb, later_weights):
    bsz, seq, d = h.shape
    steps = bsz * (seq // TM)

    def whole(a):
        return pl.BlockSpec(a.shape, lambda b, s: (0,) * a.ndim)

    def row_slab(a):
        layers, rows, cols = a.shape
        assert rows % (steps * 2 * SUBLANES) == 0
        return pl.BlockSpec((layers, rows // steps, cols), lambda b, s: (0, b * (seq // TM) + s, 0))

    for (_, prev_dil), (_, dil) in zip(DILATED_GROUPS, DILATED_GROUPS[1:]):
        assert dil % prev_dil == 0 and TM % (2 * SUBLANES * dil) == 0
    assert DILATED_GROUPS[0][1] == 1
    return pl.pallas_call(
        _inproj_ab_kernel,
        grid=(bsz, seq // TM),
        in_specs=[pl.BlockSpec((1, TM, d), lambda b, s: (b, s, 0))]
        + [whole(a) for a in (g, w, cw, cb, lng, lnb)]
        + [row_slab(a) for a in later_weights],
        out_specs=[pl.BlockSpec((1, TM, CONV_A_CH), lambda b, s: (b, s, 0))]
        + [pl.BlockSpec((1, dil, TM // dil, 3 * ATTN_OUT), lambda b, s: (b, 0, s, 0))
           for _, dil in DILATED_GROUPS]
        + [row_slab(a) for a in later_weights],
        out_shape=[jax.ShapeDtypeStruct((bsz, seq, CONV_A_CH), BF16)]
        + [jax.ShapeDtypeStruct((bsz, dil, seq // dil, 3 * ATTN_OUT), BF16)
           for _, dil in DILATED_GROUPS]
        + [jax.ShapeDtypeStruct(a.shape, BF16) for a in later_weights],
        scratch_shapes=[
            pltpu.VMEM((CONV_A_CH // LANES, CONV_HALO + TM, LANES), F32),
            pltpu.VMEM((N_GROUPS, TM, D_MODEL), BF16),
            pltpu.VMEM((2, D_MODEL // LANES, TM, LANES), F32),
        ],
        compiler_params=pltpu.CompilerParams(
            dimension_semantics=("arbitrary", "arbitrary"),
            vmem_limit_bytes=VMEM_LIMIT),
        name="inproj_ab",
    )(h, g, w, cw, cb, lng, lnb, *later_weights)


def _attn_kernel(bucket_ref, rel_ref, q_ref, k_ref, kp_ref, v_ref, vp_ref, o_ref, lse_ref,
                 bias_scr, *, head0):
    pair = 2 * STEPS

    @pl.when((pl.program_id(0) == 0) & (pl.program_id(1) == 0) & (pl.program_id(2) == 0))
    def _():
        bucket = bucket_ref[...]
        for h in range(HEADS_PER_GROUP):
            acc = jnp.full((STEPS, pair), NEG_INF, F32)
            for b in range(NUM_BUCKETS):
                acc = jnp.where(bucket == b, rel_ref[b, head0 + h], acc)
            bias_scr[h * STEPS:(h + 1) * STEPS, :] = acc

    first_tile = pl.program_id(2) == 0
    lane = lax.broadcasted_iota(jnp.int32, (STEPS, LANES), 1)
    keep_lo = jnp.where(lax.broadcasted_iota(jnp.int32, (1, LANES), 1) < HEAD_DIM, 1.0, 0.0).astype(BF16)
    keep_hi = jnp.where(lax.broadcasted_iota(jnp.int32, (1, LANES), 1) < HEAD_DIM, 0.0, 1.0).astype(BF16)
    ones = jnp.ones((pair, LANES), BF16)
    _, n_res, n_rows, _ = q_ref.shape
    for rr, i in [(rr, i) for rr in range(n_res) for i in range(n_rows // STEPS)]:
        rows = slice(i * STEPS, (i + 1) * STEPS)
        prev = slice((i - 1) * STEPS, i * STEPS)
        scores = []
        for j in range(HEADS_PER_GROUP // 2):
            cols = slice(j * LANES, (j + 1) * LANES)
            qp = q_ref[0, rr, rows, cols]
            q2 = jnp.concatenate([qp * keep_lo, qp * keep_hi], axis=0)
            kprev = kp_ref[0, rr, :, cols] if i == 0 else k_ref[0, rr, prev, cols]
            k2 = jnp.concatenate([kprev, k_ref[0, rr, rows, cols]], axis=0)
            scores.append(_dot_nt(q2, k2) + bias_scr[j * pair:(j + 1) * pair, :])
        probs, maxes = [], []
        for j in range(HEADS_PER_GROUP // 2):
            sp, sc = scores[j][:, :STEPS], scores[j][:, STEPS:]
            if i == 0:
                sp = jnp.where(first_tile, NEG_INF, sp)
            mx = jnp.max(jnp.maximum(sp, sc), axis=-1, keepdims=True)
            probs.append(jnp.concatenate([jnp.exp(sp - mx), jnp.exp(sc - mx)], axis=1).astype(BF16))
            maxes.append(mx)
        tile_m = jnp.zeros((STEPS, LANES), F32)
        tile_den = jnp.ones((STEPS, LANES), F32)
        for j in range(HEADS_PER_GROUP // 2):
            cols = slice(j * LANES, (j + 1) * LANES)
            vprev = vp_ref[0, rr, :, cols] if i == 0 else v_ref[0, rr, prev, cols]
            v2 = jnp.concatenate([vprev, v_ref[0, rr, rows, cols]], axis=0)
            o = _dot(probs[j], jnp.concatenate([v2, ones], axis=1))
            den = o[:, LANES:]
            on = o[:, :LANES] / den
            o_ref[0, rr, rows, cols] = jnp.where(lane < HEAD_DIM, on[:STEPS], on[STEPS:]).astype(BF16)
            for e in range(2):
                part = slice(e * STEPS, (e + 1) * STEPS)
                tile_m = jnp.where(lane == 2 * j + e, maxes[j][part], tile_m)
                tile_den = jnp.where(lane == 2 * j + e, den[part], tile_den)
        lse_ref[0, rr, rows, :] = tile_m + jnp.log(tile_den)


def _attention_group(qkv, bucket, rel_bias, g):
    bsz, dilation, sd, _ = qkv.shape
    n_rows = min(sd, ATTN_STEP_ROWS)
    n_res = ATTN_STEP_ROWS // n_rows
    assert sd % n_rows == 0 and dilation % n_res == 0 and n_rows % STEPS == 0
    per_tile = n_rows // STEPS

    def cur(col):
        return pl.BlockSpec((1, n_res, n_rows, ATTN_OUT), lambda b, r, t: (b, r, t, col))

    def prev(col):
        return pl.BlockSpec((1, n_res, STEPS, ATTN_OUT),
                            lambda b, r, t: (b, r, jnp.maximum(t * per_tile - 1, 0), col))

    return pl.pallas_call(
        functools.partial(_attn_kernel, head0=g * HEADS_PER_GROUP),
        grid=(bsz, dilation // n_res, sd // n_rows),
        in_specs=[
            pl.BlockSpec(bucket.shape, lambda b, r, t: (0, 0)),
            pl.BlockSpec(memory_space=pltpu.SMEM),
            cur(0), cur(1), prev(1), cur(2), prev(2),
        ],
        out_specs=[
            pl.BlockSpec((1, n_res, n_rows, ATTN_OUT), lambda b, r, t: (b, r, t, 0)),
            pl.BlockSpec((1, n_res, n_rows, LANES), lambda b, r, t: (b, r, t, 0)),
        ],
        out_shape=[
            jax.ShapeDtypeStruct((bsz, dilation, sd, ATTN_OUT), BF16),
            jax.ShapeDtypeStruct((bsz, dilation, sd, LANES), F32),
        ],
        scratch_shapes=[pltpu.VMEM((HEADS_PER_GROUP * STEPS, 2 * STEPS), F32)],
        compiler_params=pltpu.CompilerParams(
            dimension_semantics=("arbitrary", "arbitrary", "arbitrary"),
            vmem_limit_bytes=VMEM_LIMIT),
        name=f"attn_d{dilation}",
    )(bucket, rel_bias, qkv, qkv, qkv, qkv, qkv)


def _t5_causal_bucket(n):
    max_exact = NUM_BUCKETS // 2
    nf = jnp.maximum(n, 1).astype(F32)
    large = max_exact + (jnp.log(nf / max_exact) / math.log(REL_MAX_DISTANCE / max_exact)
                         * (NUM_BUCKETS - max_exact)).astype(jnp.int32)
    return jnp.where(n < max_exact, n, jnp.minimum(large, NUM_BUCKETS - 1))


def _band_buckets(dilation):
    a_idx = jnp.arange(STEPS)[:, None]
    c_idx = jnp.arange(2 * STEPS)[None, :]
    m = a_idx + STEPS - c_idx
    band = (m >= 0) & (m <= STEPS)
    return jnp.where(band, _t5_causal_bucket(jnp.clip(m, 0, STEPS) * dilation), -1)


def _ab_tail_kernel(ya_ref, o1_ref, o2_ref, o3_ref, l1_ref, l2_ref, l3_ref, h_ref,
                    wout_ref, mg_ref, wup_ref, wdown_ref, out_ref, onat, lnat, z_scr):
    strided = [(o_ref, l_ref, dil) for (o_ref, l_ref, (_, dil))
               in zip((o1_ref, o2_ref, o3_ref), (l1_ref, l2_ref, l3_ref), DILATED_GROUPS) if dil > 1]
    for n, (o_ref, l_ref, dil) in enumerate(strided):
        rows = TM // dil
        for r in range(dil):
            lnat[n, pl.ds(r, rows, stride=dil), :] = l_ref[0, r]
            for s in range(SLABS):
                onat[n, s, pl.ds(r, rows, stride=dil), :] = (
                    o_ref[0, r, :, s * LANES:(s + 1) * LANES].astype(F32))

    outs, lses, n = [], [], 0
    for o_ref, l_ref, (_, dil) in zip((o1_ref, o2_ref, o3_ref), (l1_ref, l2_ref, l3_ref),
                                      DILATED_GROUPS):
        if dil == 1:
            outs.append(o_ref[0, 0].astype(F32))
            lses.append(l_ref[0, 0])
        else:
            outs.append(jnp.concatenate([onat[n, s] for s in range(SLABS)], axis=1))
            lses.append(lnat[n])
            n += 1

    mx = jnp.maximum(jnp.maximum(lses[0], lses[1]), lses[2])
    es = [jnp.exp(l - mx) for l in lses]
    inv = 1.0 / (es[0] + es[1] + es[2])
    head_of_col = lax.broadcasted_iota(jnp.int32, (2 * LANES, ATTN_OUT), 1) // HEAD_DIM
    row = lax.broadcasted_iota(jnp.int32, (2 * LANES, ATTN_OUT), 0) % LANES
    spread = jnp.where(head_of_col == row, 1.0, 0.0).astype(BF16)

    def expand(w):
        hi = w.astype(BF16)
        lo = (w - hi.astype(F32)).astype(BF16)
        return _dot(jnp.concatenate([hi, lo], axis=1), spread)

    yb = outs[-1]
    for e, o in zip(es[:-1], outs[:-1]):
        yb = yb + expand(e * inv) * (o - outs[-1])
    y = _dot(ya_ref[0], wout_ref[:CONV_A_CH, :]) + _dot(yb.astype(BF16), wout_ref[CONV_A_CH:, :])
    out_ref[0] = _mlp(h_ref[0] + y, mg_ref, wup_ref, wdown_ref, z_scr)


def _layer_weight(a, layer):
    return pl.BlockSpec((None,) + a.shape[1:], lambda b, s: (layer, 0, 0),
                        pipeline_mode=pl.Buffered(1))


def _ab_tail(ya, outs, lses, h, wout, mg, wup, wdown, layer):
    bsz, seq, d = h.shape

    def tile(width):
        return pl.BlockSpec((1, TM, width), lambda b, s: (b, s, 0))

    def residue_tile(dil, width):
        return pl.BlockSpec((1, dil, TM // dil, width), lambda b, s: (b, 0, s, 0))

    def whole(a):
        return pl.BlockSpec(a.shape, lambda b, s: (0,) * a.ndim)

    n_strided = sum(1 for _, dil in DILATED_GROUPS if dil > 1)
    return pl.pallas_call(
        _ab_tail_kernel,
        grid=(bsz, seq // TM),
        in_specs=[tile(CONV_A_CH)]
        + [residue_tile(dil, ATTN_OUT) for _, dil in DILATED_GROUPS]
        + [residue_tile(dil, LANES) for _, dil in DILATED_GROUPS]
        + [tile(d), _layer_weight(wout, 0), whole(mg), _layer_weight(wup, layer),
           _layer_weight(wdown, layer)],
        out_specs=tile(d),
        out_shape=jax.ShapeDtypeStruct((bsz, seq, d), F32),
        scratch_shapes=[
            pltpu.VMEM((n_strided, SLABS, TM, LANES), F32),
            pltpu.VMEM((n_strided, TM, LANES), F32),
            pltpu.VMEM((TM, D_FF), BF16),
        ],
        compiler_params=pltpu.CompilerParams(
            dimension_semantics=("arbitrary", "arbitrary"),
            vmem_limit_bytes=VMEM_LIMIT),
        name="ab_tail",
    )(ya, *outs, *lses, h, wout, mg, wup, wdown)


def _sc_layer_kernel(h_ref, ng_ref, win_ref, cw_ref, wout_ref, mg_ref, wup_ref, wdown_ref,
                     fg_ref, out_ref, cvbuf, z_scr):
    halo = SUBLANES

    @pl.when(pl.program_id(1) == 0)
    def _():
        cvbuf[0:halo, :] = jnp.zeros((halo, D_MODEL), F32)

    h = h_ref[0]
    xn = _rms_norm(h, ng_ref[...]).astype(BF16)
    c_gate = _dot(xn, win_ref[:, D_MODEL:2 * D_MODEL])
    val = _dot(xn, win_ref[:, 2 * D_MODEL:])
    cvbuf[halo:halo + TM, :] = c_gate * val
    first = halo - (SC_CONV_WIDTH - 1)
    conv = cw_ref[0:1, :] * cvbuf[first:first + TM, :]
    for k in range(1, SC_CONV_WIDTH):
        conv = conv + cw_ref[k:k + 1, :] * cvbuf[first + k:first + k + TM, :]
    cvbuf[0:halo, :] = cvbuf[TM:TM + halo, :]
    b_gate = _dot(xn, win_ref[:, :D_MODEL])
    h1 = h + _dot((b_gate * conv).astype(BF16), wout_ref[...])
    h2 = _mlp(h1, mg_ref, wup_ref, wdown_ref, z_scr)
    out_ref[0] = _rms_norm(h2, fg_ref[...])


def _sc_layer(h, ng, win, cw, wout, mg, wup, wdown, fg, layer):
    bsz, seq, d = h.shape

    def whole(a):
        return pl.BlockSpec(a.shape, lambda b, s: (0,) * a.ndim)

    tile = pl.BlockSpec((1, TM, d), lambda b, s: (b, s, 0))
    return pl.pallas_call(
        _sc_layer_kernel,
        grid=(bsz, seq // TM),
        in_specs=[tile, whole(ng), _layer_weight(win, 0), whole(cw), _layer_weight(wout, 0),
                  whole(mg), _layer_weight(wup, layer), _layer_weight(wdown, layer), whole(fg)],
        out_specs=tile,
        out_shape=jax.ShapeDtypeStruct((bsz, seq, d), F32),
        scratch_shapes=[
            pltpu.VMEM((SUBLANES + TM, d), F32),
            pltpu.VMEM((TM, D_FF), BF16),
        ],
        compiler_params=pltpu.CompilerParams(
            dimension_semantics=("arbitrary", "arbitrary"),
            vmem_limit_bytes=VMEM_LIMIT),
        name="sc_layer",
    )(h, ng, win, cw, wout, mg, wup, wdown, fg)


def kernel(x, rel_bias, ab_norm, ab_w_in, ab_conv_w, ab_conv_b, ab_ln_g, ab_ln_b, ab_w_out,
           sc_norm, sc_w_in, sc_conv_w, sc_w_out, mlp_norm, mlp_w_up, mlp_w_down, final_norm):
    assert x.shape[1] % (STEPS * DILATED_GROUPS[-1][1]) == 0 and x.shape[2] == D_MODEL
    assert ab_norm.shape[0] == 1 and sc_norm.shape[0] == 1 and mlp_norm.shape[0] == 2

    def row(v):
        return v.reshape(1, -1).astype(F32)

    q_cols = (jnp.arange(ab_w_in.shape[-1]) >= 2 * CONV_A_CH) & (
        jnp.arange(ab_w_in.shape[-1]) < 2 * CONV_A_CH + N_GROUPS * ATTN_OUT)
    w_in = (ab_w_in[0] * jnp.where(q_cols, HEAD_DIM ** -0.5, 1.0)).astype(BF16)
    later = (ab_w_out, mlp_w_up, mlp_w_down, sc_w_in, sc_w_out)
    ya, *rest = _inproj_ab(x, row(ab_norm[0]), w_in, ab_conv_w[0].astype(F32),
                           row(ab_conv_b[0]), row(ab_ln_g[0]), row(ab_ln_b[0]),
                           [w.astype(F32) for w in later])
    qkvs, (w_out0, w_up, w_down, w_in1, w_out1) = rest[:N_GROUPS], rest[N_GROUPS:]
    outs, lses = [], []
    for g, (_, dilation) in enumerate(DILATED_GROUPS):
        o, lse = _attention_group(qkvs[g], _band_buckets(dilation), rel_bias.astype(F32), g)
        outs.append(o)
        lses.append(lse)
    h = _ab_tail(ya, outs, lses, x, w_out0, row(mlp_norm[0]), w_up, w_down, layer=0)
    return _sc_layer(h, row(sc_norm[0]), w_in1, sc_conv_w[0].astype(F32), w_out1,
                     row(mlp_norm[1]), w_up, w_down, row(final_norm), layer=1)
```

```python
import functools
import math

import jax
import jax.numpy as jnp
from jax import lax
from jax.experimental import pallas as pl
from jax.experimental.pallas import tpu as pltpu

D_MODEL = 1024
CONV_A_CH = 512
CONV_A_WIDTH = 31
LN_EPS = 1e-5
HEAD_DIM = 64
HEADS_PER_GROUP = 8
DILATED_GROUPS = ((128, 1), (512, 4), (2048, 16))
N_GROUPS = 3
ATTN_OUT = HEADS_PER_GROUP * HEAD_DIM
NUM_BUCKETS = 32
REL_MAX_DISTANCE = 2048
SC_CONV_WIDTH = 3
D_FF = 4 * D_MODEL
RMS_EPS = 1e-6
NEG_INF = -1e30

STEPS = 128
LANES = 128
SUBLANES = 8
VMEM_LIMIT = 56 * 1024 * 1024

TM = 512
ATTN_STEP_ROWS = 2048
CONV_HALO = 32
CONV_ROWS = 64
FF_CHUNK = 1024
SLABS = ATTN_OUT // LANES

BF16 = jnp.bfloat16
F32 = jnp.float32


def _dot(a, b):
    return jnp.dot(a, b, preferred_element_type=F32)


def _dot_nt(a, b):
    return lax.dot_general(a, b, (((1,), (1,)), ((), ())), preferred_element_type=F32)


def _rms_norm(x, g):
    y = x * lax.rsqrt(jnp.mean(x * x, axis=-1, keepdims=True) + RMS_EPS)
    return y * g


def _mlp(h, g_ref, wup_ref, wdown_ref, z_scr):
    xn = _rms_norm(h, g_ref[...]).astype(BF16)
    for c in range(D_FF // FF_CHUNK):
        cols = slice(c * FF_CHUNK, (c + 1) * FF_CHUNK)
        z = jnp.maximum(_dot(xn, wup_ref[:, cols]), 0.0)
        z_scr[:, cols] = (z * z).astype(BF16)
    return h + _dot(z_scr[...], wdown_ref[...])


def _conv_chunk(r0, hbuf, cw_ref, cb_ref, lng_ref, lnb_ref, ya_ref):
    lead = CONV_HALO - (CONV_A_WIDTH - 1)
    nblk = CONV_ROWS // SUBLANES
    accs = []
    for s in range(CONV_A_CH // LANES):
        cols = slice(s * LANES, (s + 1) * LANES)
        taps = [jnp.broadcast_to(cw_ref[k:k + 1, cols], (SUBLANES, LANES)) for k in range(CONV_A_WIDTH)]
        blk = [jnp.broadcast_to(cb_ref[:, cols], (SUBLANES, LANES)) for _ in range(nblk)]
        for o in range(SUBLANES * (nblk - 1) + CONV_A_WIDTH):
            win = hbuf[s, r0 + lead + o:r0 + lead + o + SUBLANES, :]
            for i in range(nblk):
                k = o - SUBLANES * i
                if 0 <= k < CONV_A_WIDTH:
                    blk[i] = blk[i] + taps[k] * win
        accs.append(jnp.concatenate(blk, axis=0))
    acc = jnp.concatenate(accs, axis=1)
    mu = jnp.mean(acc, axis=-1, keepdims=True)
    cen = acc - mu
    var = jnp.mean(cen * cen, axis=-1, keepdims=True)
    y = cen * lax.rsqrt(var + LN_EPS) * lng_ref[...] + lnb_ref[...]
    ya_ref[0, r0:r0 + CONV_ROWS, :] = (y * jax.nn.sigmoid(y)).astype(BF16)


def _residue_major_inputs(xf, xn_perm, xslab):
    n_slabs = D_MODEL // LANES
    xn_perm[0] = xf.astype(BF16)
    for s in range(n_slabs):
        xslab[0, s] = xf[:, s * LANES:(s + 1) * LANES]
    residues = [[0]]
    for g in range(1, N_GROUPS):
        prev_dil, dil = DILATED_GROUPS[g - 1][1], DILATED_GROUPS[g][1]
        ratio, rows = dil // prev_dil, TM // dil
        src, dst = (g - 1) % 2, g % 2
        res = []
        for blk, blk_res in enumerate(residues[-1]):
            for r in range(ratio):
                b = blk * ratio + r
                res.append(blk_res + prev_dil * r)
                for s in range(n_slabs):
                    piece = xslab[src, s, pl.ds(blk * (TM // prev_dil) + r, rows, stride=ratio), :]
                    xn_perm[g, b * rows:(b + 1) * rows, s * LANES:(s + 1) * LANES] = piece.astype(BF16)
                    if g + 1 < N_GROUPS:
                        xslab[dst, s, b * rows:(b + 1) * rows, :] = piece
        residues.append(res)
    return residues


def _qkv_chunk(xn_perm, w_ref, which, g, residues, out_ref):
    lo = 2 * CONV_A_CH + (which * N_GROUPS + g) * ATTN_OUT
    acc = _dot(xn_perm[g], w_ref[:, lo:lo + ATTN_OUT]).astype(BF16)
    rows = TM // len(residues)
    for b, r in enumerate(residues):
        out_ref[0, r, :, which * ATTN_OUT:(which + 1) * ATTN_OUT] = acc[b * rows:(b + 1) * rows]


def _inproj_ab_kernel(x_ref, g_ref, w_ref, cw_ref, cb_ref, lng_ref, lnb_ref, *rest):
    n_later = (len(rest) - 7) // 2
    later_f32, rest = rest[:n_later], rest[n_later:]
    ya_ref, qkv1_ref, qkv2_ref, qkv3_ref = rest[:4]
    later_bf16, (hbuf, xn_perm, xslab) = rest[4:4 + n_later], rest[4 + n_later:]
    for src, dst in zip(later_f32, later_bf16):
        dst[...] = src[...].astype(BF16)

    @pl.when(pl.program_id(1) == 0)
    def _():
        hbuf[:, 0:CONV_HALO, :] = jnp.zeros((CONV_A_CH // LANES, CONV_HALO, LANES), F32)

    xf = _rms_norm(x_ref[0], g_ref[...])
    xn = xf.astype(BF16)
    val = _dot(xn, w_ref[:, :CONV_A_CH])
    gate = _dot(xn, w_ref[:, CONV_A_CH:2 * CONV_A_CH])
    glu = val * jax.nn.sigmoid(gate)
    for s in range(CONV_A_CH // LANES):
        hbuf[s, CONV_HALO:CONV_HALO + TM, :] = glu[:, s * LANES:(s + 1) * LANES]
    residues = _residue_major_inputs(xf, xn_perm, xslab)

    for r0 in range(0, TM, CONV_ROWS):
        _conv_chunk(r0, hbuf, cw_ref, cb_ref, lng_ref, lnb_ref, ya_ref)
    hbuf[:, 0:CONV_HALO, :] = hbuf[:, TM:TM + CONV_HALO, :]

    outs = (qkv1_ref, qkv2_ref, qkv3_ref)
    for which in range(3):
        for g in range(N_GROUPS):
            _qkv_chunk(xn_perm, w_ref, which, g, residues[g], outs[g])


def _inproj_ab(h, g, w, cw, cb, lng, lnb, later_weights):
    bsz, seq, d = h.shape
    steps = bsz * (seq // TM)

    def whole(a):
        return pl.BlockSpec(a.shape, lambda b, s: (0,) * a.ndim)

    def row_slab(a):
        layers, rows, cols = a.shape
        assert rows % (steps * 2 * SUBLANES) == 0
        return pl.BlockSpec((layers, rows // steps, cols), lambda b, s: (0, b * (seq // TM) + s, 0))

    for (_, prev_dil), (_, dil) in zip(DILATED_GROUPS, DILATED_GROUPS[1:]):
        assert dil % prev_dil == 0 and TM % (2 * SUBLANES * dil) == 0
    assert DILATED_GROUPS[0][1] == 1
    return pl.pallas_call(
        _inproj_ab_kernel,
        grid=(bsz, seq // TM),
        in_specs=[pl.BlockSpec((1, TM, d), lambda b, s: (b, s, 0))]
        + [whole(a) for a in (g, w, cw, cb, lng, lnb)]
        + [row_slab(a) for a in later_weights],
        out_specs=[pl.BlockSpec((1, TM, CONV_A_CH), lambda b, s: (b, s, 0))]
        + [pl.BlockSpec((1, dil, TM // dil, 3 * ATTN_OUT), lambda b, s: (b, 0, s, 0))
           for _, dil in DILATED_GROUPS]
        + [row_slab(a) for a in later_weights],
        out_shape=[jax.ShapeDtypeStruct((bsz, seq, CONV_A_CH), BF16)]
        + [jax.ShapeDtypeStruct((bsz, dil, seq // dil, 3 * ATTN_OUT), BF16)
           for _, dil in DILATED_GROUPS]
        + [jax.ShapeDtypeStruct(a.shape, BF16) for a in later_weights],
        scratch_shapes=[
            pltpu.VMEM((CONV_A_CH // LANES, CONV_HALO + TM, LANES), F32),
            pltpu.VMEM((N_GROUPS, TM, D_MODEL), BF16),
            pltpu.VMEM((2, D_MODEL // LANES, TM, LANES), F32),
        ],
        compiler_params=pltpu.CompilerParams(
            dimension_semantics=("arbitrary", "arbitrary"),
            vmem_limit_bytes=VMEM_LIMIT),
        name="inproj_ab",
    )(h, g, w, cw, cb, lng, lnb, *later_weights)


def _attn_kernel(bucket_ref, rel_ref, q_ref, k_ref, kp_ref, v_ref, vp_ref, o_ref, lse_ref,
                 bias_scr, *, head0):
    pair = 2 * STEPS

    @pl.when((pl.program_id(0) == 0) & (pl.program_id(1) == 0) & (pl.program_id(2) == 0))
    def _():
        bucket = bucket_ref[...]
        for h in range(HEADS_PER_GROUP):
            acc = jnp.full((STEPS, pair), NEG_INF, F32)
            for b in range(NUM_BUCKETS):
                acc = jnp.where(bucket == b, rel_ref[b, head0 + h], acc)
            bias_scr[h * STEPS:(h + 1) * STEPS, :] = acc

    first_tile = pl.program_id(2) == 0
    lane = lax.broadcasted_iota(jnp.int32, (STEPS, LANES), 1)
    keep_lo = jnp.where(lax.broadcasted_iota(jnp.int32, (1, LANES), 1) < HEAD_DIM, 1.0, 0.0).astype(BF16)
    keep_hi = jnp.where(lax.broadcasted_iota(jnp.int32, (1, LANES), 1) < HEAD_DIM, 0.0, 1.0).astype(BF16)
    ones = jnp.ones((pair, LANES), BF16)
    _, n_res, n_rows, _ = q_ref.shape
    for rr, i in [(rr, i) for rr in range(n_res) for i in range(n_rows // STEPS)]:
        rows = slice(i * STEPS, (i + 1) * STEPS)
        prev = slice((i - 1) * STEPS, i * STEPS)
        scores = []
        for j in range(HEADS_PER_GROUP // 2):
            cols = slice(j * LANES, (j + 1) * LANES)
            qp = q_ref[0, rr, rows, cols]
            q2 = jnp.concatenate([qp * keep_lo, qp * keep_hi], axis=0)
            kprev = kp_ref[0, rr, :, cols] if i == 0 else k_ref[0, rr, prev, cols]
            k2 = jnp.concatenate([kprev, k_ref[0, rr, rows, cols]], axis=0)
            scores.append(_dot_nt(q2, k2) + bias_scr[j * pair:(j + 1) * pair, :])
        probs, maxes = [], []
        for j in range(HEADS_PER_GROUP // 2):
            sp, sc = scores[j][:, :STEPS], scores[j][:, STEPS:]
            if i == 0:
                sp = jnp.where(first_tile, NEG_INF, sp)
            mx = jnp.max(jnp.maximum(sp, sc), axis=-1, keepdims=True)
            probs.append(jnp.concatenate([jnp.exp(sp - mx), jnp.exp(sc - mx)], axis=1).astype(BF16))
            maxes.append(mx)
        tile_m = jnp.zeros((STEPS, LANES), F32)
        tile_den = jnp.ones((STEPS, LANES), F32)
        for j in range(HEADS_PER_GROUP // 2):
            cols = slice(j * LANES, (j + 1) * LANES)
            vprev = vp_ref[0, rr, :, cols] if i == 0 else v_ref[0, rr, prev, cols]
            v2 = jnp.concatenate([vprev, v_ref[0, rr, rows, cols]], axis=0)
            o = _dot(probs[j], jnp.concatenate([v2, ones], axis=1))
            den = o[:, LANES:]
            on = o[:, :LANES] / den
            o_ref[0, rr, rows, cols] = jnp.where(lane < HEAD_DIM, on[:STEPS], on[STEPS:]).astype(BF16)
            for e in range(2):
                part = slice(e * STEPS, (e + 1) * STEPS)
                tile_m = jnp.where(lane == 2 * j + e, maxes[j][part], tile_m)
                tile_den = jnp.where(lane == 2 * j + e, den[part], tile_den)
        lse_ref[0, rr, rows, :] = tile_m + jnp.log(tile_den)


def _attention_group(qkv, bucket, rel_bias, g):
    bsz, dilation, sd, _ = qkv.shape
    n_rows = min(sd, ATTN_STEP_ROWS)
    n_res = ATTN_STEP_ROWS // n_rows
    assert sd % n_rows == 0 and dilation % n_res == 0 and n_rows % STEPS == 0
    per_tile = n_rows // STEPS

    def cur(col):
        return pl.BlockSpec((1, n_res, n_rows, ATTN_OUT), lambda b, r, t: (b, r, t, col))

    def prev(col):
        return pl.BlockSpec((1, n_res, STEPS, ATTN_OUT),
                            lambda b, r, t: (b, r, jnp.maximum(t * per_tile - 1, 0), col))

    return pl.pallas_call(
        functools.partial(_attn_kernel, head0=g * HEADS_PER_GROUP),
        grid=(bsz, dilation // n_res, sd // n_rows),
        in_specs=[
            pl.BlockSpec(bucket.shape, lambda b, r, t: (0, 0)),
            pl.BlockSpec(memory_space=pltpu.SMEM),
            cur(0), cur(1), prev(1), cur(2), prev(2),
        ],
        out_specs=[
            pl.BlockSpec((1, n_res, n_rows, ATTN_OUT), lambda b, r, t: (b, r, t, 0)),
            pl.BlockSpec((1, n_res, n_rows, LANES), lambda b, r, t: (b, r, t, 0)),
        ],
        out_shape=[
            jax.ShapeDtypeStruct((bsz, dilation, sd, ATTN_OUT), BF16),
            jax.ShapeDtypeStruct((bsz, dilation, sd, LANES), F32),
        ],
        scratch_shapes=[pltpu.VMEM((HEADS_PER_GROUP * STEPS, 2 * STEPS), F32)],
        compiler_params=pltpu.CompilerParams(
            dimension_semantics=("arbitrary", "arbitrary", "arbitrary"),
            vmem_limit_bytes=VMEM_LIMIT),
        name=f"attn_d{dilation}",
    )(bucket, rel_bias, qkv, qkv, qkv, qkv, qkv)


def _t5_causal_bucket(n):
    max_exact = NUM_BUCKETS // 2
    nf = jnp.maximum(n, 1).astype(F32)
    large = max_exact + (jnp.log(nf / max_exact) / math.log(REL_MAX_DISTANCE / max_exact)
                         * (NUM_BUCKETS - max_exact)).astype(jnp.int32)
    return jnp.where(n < max_exact, n, jnp.minimum(large, NUM_BUCKETS - 1))


def _band_buckets(dilation):
    a_idx = jnp.arange(STEPS)[:, None]
    c_idx = jnp.arange(2 * STEPS)[None, :]
    m = a_idx + STEPS - c_idx
    band = (m >= 0) & (m <= STEPS)
    return jnp.where(band, _t5_causal_bucket(jnp.clip(m, 0, STEPS) * dilation), -1)


def _ab_tail_kernel(ya_ref, o1_ref, o2_ref, o3_ref, l1_ref, l2_ref, l3_ref, h_ref,
                    wout_ref, mg_ref, wup_ref, wdown_ref, out_ref, nat, tmp, z_scr):
    outs, stats = [o1_ref[0, 0].astype(F32)], [l1_ref[0, 0]]
    for g in range(1, N_GROUPS):
        o_ref, l_ref = (o1_ref, o2_ref, o3_ref)[g], (l1_ref, l2_ref, l3_ref)[g]
        dil = DILATED_GROUPS[g][1]
        blocks = {r: (lambda s, r=r: l_ref[0, r] if s == SLABS else
                      o_ref[0, r, :, s * LANES:(s + 1) * LANES].astype(F32)) for r in range(dil)}
        for level in range(g, 0, -1):
            prev_dil = DILATED_GROUPS[level - 1][1]
            ratio, rows = dil // prev_dil, TM // dil
            merged = {}
            for r_new in range(prev_dil):
                base = r_new * (TM // prev_dil)
                for sub in range(ratio):
                    for s in range(SLABS + 1):
                        piece = blocks[sub * prev_dil + r_new](s)
                        if level == 1:
                            nat[g - 1, s, pl.ds(sub, rows, stride=ratio), :] = piece
                        else:
                            tmp[s, pl.ds(base + sub, rows, stride=ratio), :] = piece
                merged[r_new] = (lambda s, base=base, n=TM // prev_dil: tmp[s, base:base + n, :])
            blocks, dil = merged, prev_dil
        outs.append(jnp.concatenate([nat[g - 1, s] for s in range(SLABS)], axis=1))
        stats.append(nat[g - 1, SLABS])

    mx = jnp.maximum(jnp.maximum(stats[0], stats[1]), stats[2])
    es = [jnp.exp(l - mx) for l in stats]
    inv = 1.0 / (es[0] + es[1] + es[2])
    head_of_col = lax.broadcasted_iota(jnp.int32, (2 * LANES, ATTN_OUT), 1) // HEAD_DIM
    row = lax.broadcasted_iota(jnp.int32, (2 * LANES, ATTN_OUT), 0) % LANES
    spread = jnp.where(head_of_col == row, 1.0, 0.0).astype(BF16)

    def expand(w):
        hi = w.astype(BF16)
        lo = (w - hi.astype(F32)).astype(BF16)
        return _dot(jnp.concatenate([hi, lo], axis=1), spread)

    yb = outs[-1]
    for e, o in zip(es[:-1], outs[:-1]):
        yb = yb + expand(e * inv) * (o - outs[-1])
    y = _dot(ya_ref[0], wout_ref[:CONV_A_CH, :]) + _dot(yb.astype(BF16), wout_ref[CONV_A_CH:, :])
    out_ref[0] = _mlp(h_ref[0] + y, mg_ref, wup_ref, wdown_ref, z_scr)


def _layer_weight(a, layer):
    return pl.BlockSpec((None,) + a.shape[1:], lambda b, s: (layer, 0, 0),
                        pipeline_mode=pl.Buffered(1))


def _ab_tail(ya, outs, stats, h, wout, mg, wup, wdown, layer):
    bsz, seq, d = h.shape

    def tile(width):
        return pl.BlockSpec((1, TM, width), lambda b, s: (b, s, 0))

    def residue_tile(dil, width):
        return pl.BlockSpec((1, dil, TM // dil, width), lambda b, s: (b, 0, s, 0))

    def whole(a):
        return pl.BlockSpec(a.shape, lambda b, s: (0,) * a.ndim)

    return pl.pallas_call(
        _ab_tail_kernel,
        grid=(bsz, seq // TM),
        in_specs=[tile(CONV_A_CH)]
        + [residue_tile(dil, ATTN_OUT) for _, dil in DILATED_GROUPS]
        + [residue_tile(dil, LANES) for _, dil in DILATED_GROUPS]
        + [tile(d), _layer_weight(wout, 0), whole(mg), _layer_weight(wup, layer),
           _layer_weight(wdown, layer)],
        out_specs=tile(d),
        out_shape=jax.ShapeDtypeStruct((bsz, seq, d), F32),
        scratch_shapes=[
            pltpu.VMEM((N_GROUPS - 1, SLABS + 1, TM, LANES), F32),
            pltpu.VMEM((SLABS + 1, TM, LANES), F32),
            pltpu.VMEM((TM, D_FF), BF16),
        ],
        compiler_params=pltpu.CompilerParams(
            dimension_semantics=("arbitrary", "arbitrary"),
            vmem_limit_bytes=VMEM_LIMIT),
        name="ab_tail",
    )(ya, *outs, *stats, h, wout, mg, wup, wdown)


def _sc_layer_kernel(h_ref, ng_ref, win_ref, cw_ref, wout_ref, mg_ref, wup_ref, wdown_ref,
                     fg_ref, out_ref, cvbuf, z_scr):
    halo = SUBLANES

    @pl.when(pl.program_id(1) == 0)
    def _():
        cvbuf[0:halo, :] = jnp.zeros((halo, D_MODEL), F32)

    h = h_ref[0]
    xn = _rms_norm(h, ng_ref[...]).astype(BF16)
    c_gate = _dot(xn, win_ref[:, D_MODEL:2 * D_MODEL])
    val = _dot(xn, win_ref[:, 2 * D_MODEL:])
    cvbuf[halo:halo + TM, :] = c_gate * val
    first = halo - (SC_CONV_WIDTH - 1)
    conv = cw_ref[0:1, :] * cvbuf[first:first + TM, :]
    for k in range(1, SC_CONV_WIDTH):
        conv = conv + cw_ref[k:k + 1, :] * cvbuf[first + k:first + k + TM, :]
    cvbuf[0:halo, :] = cvbuf[TM:TM + halo, :]
    b_gate = _dot(xn, win_ref[:, :D_MODEL])
    h1 = h + _dot((b_gate * conv).astype(BF16), wout_ref[...])
    h2 = _mlp(h1, mg_ref, wup_ref, wdown_ref, z_scr)
    out_ref[0] = _rms_norm(h2, fg_ref[...])


def _sc_layer(h, ng, win, cw, wout, mg, wup, wdown, fg, layer):
    bsz, seq, d = h.shape

    def whole(a):
        return pl.BlockSpec(a.shape, lambda b, s: (0,) * a.ndim)

    tile = pl.BlockSpec((1, TM, d), lambda b, s: (b, s, 0))
    return pl.pallas_call(
        _sc_layer_kernel,
        grid=(bsz, seq // TM),
        in_specs=[tile, whole(ng), _layer_weight(win, 0), whole(cw), _layer_weight(wout, 0),
                  whole(mg), _layer_weight(wup, layer), _layer_weight(wdown, layer), whole(fg)],
        out_specs=tile,
        out_shape=jax.ShapeDtypeStruct((bsz, seq, d), F32),
        scratch_shapes=[
            pltpu.VMEM((SUBLANES + TM, d), F32),
            pltpu.VMEM((TM, D_FF), BF16),
        ],
        compiler_params=pltpu.CompilerParams(
            dimension_semantics=("arbitrary", "arbitrary"),
            vmem_limit_bytes=VMEM_LIMIT),
        name="sc_layer",
    )(h, ng, win, cw, wout, mg, wup, wdown, fg)


def kernel(x, rel_bias, ab_norm, ab_w_in, ab_conv_w, ab_conv_b, ab_ln_g, ab_ln_b, ab_w_out,
           sc_norm, sc_w_in, sc_conv_w, sc_w_out, mlp_norm, mlp_w_up, mlp_w_down, final_norm):
    assert x.shape[1] % (STEPS * DILATED_GROUPS[-1][1]) == 0 and x.shape[2] == D_MODEL
    assert ab_norm.shape[0] == 1 and sc_norm.shape[0] == 1 and mlp_norm.shape[0] == 2

    def row(v):
        return v.reshape(1, -1).astype(F32)

    q_cols = (jnp.arange(ab_w_in.shape[-1]) >= 2 * CONV_A_CH) & (
        jnp.arange(ab_w_in.shape[-1]) < 2 * CONV_A_CH + N_GROUPS * ATTN_OUT)
    w_in = (ab_w_in[0] * jnp.where(q_cols, HEAD_DIM ** -0.5, 1.0)).astype(BF16)
    later = (ab_w_out, mlp_w_up, mlp_w_down, sc_w_in, sc_w_out)
    ya, *rest = _inproj_ab(x, row(ab_norm[0]), w_in, ab_conv_w[0].astype(F32),
                           row(ab_conv_b[0]), row(ab_ln_g[0]), row(ab_ln_b[0]),
                           [w.astype(F32) for w in later])
    qkvs, (w_out0, w_up, w_down, w_in1, w_out1) = rest[:N_GROUPS], rest[N_GROUPS:]
    outs, stats = [], []
    for g, (_, dilation) in enumerate(DILATED_GROUPS):
        o, stat = _attention_group(qkvs[g], _band_buckets(dilation), rel_bias.astype(F32), g)
        outs.append(o)
        stats.append(stat)
    h = _ab_tail(ya, outs, stats, x, w_out0, row(mlp_norm[0]), w_up, w_down, layer=0)
    return _sc_layer(h, row(sc_norm[0]), w_in1, sc_conv_w[0].astype(F32), w_out1,
                     row(mlp_norm[1]), w_up, w_down, row(final_norm), layer=1)
```
